```python
import jax, jax.numpy as jnp
from jax import lax
import numpy as np

D_MODEL = 2048
BATCH = 8
SEQ = 2048
DEPTH = 4

N_MIXERS = 4
HEAD_DIM = 128
N_HEADS = D_MODEL // HEAD_DIM
Q_BLOCK = 128
EPS = 1e-6

FOX_FGATE_BIAS = 2.0

MLA_Q_RANK = 512
MLA_KV_RANK = 512
MLA_NOPE = 128
MLA_ROPE = 64
MLA_V = 128
ROPE_THETA = 10000.0

SGU_CHUNK = 128
SGU_WIDTH = D_MODEL
SGU_GROUP_DIM = 128
SGU_GROUPS = SGU_WIDTH // SGU_GROUP_DIM

D_FF = 5632
CONV_WIDTH = 3

kernel_name = 'hybrid_interleaved_fox_mla_stickbreak_sgu'


def _n_layers_of(m):
    return len(range(m, DEPTH, N_MIXERS))


def rms_norm(x, gain):
    x32 = x.astype(jnp.float32)
    y = x32 * lax.rsqrt(jnp.mean(x32 * x32, axis=-1, keepdims=True) + EPS)
    return (y * gain.astype(jnp.float32)).astype(x.dtype)


def _sweep_query_blocks(block_fn, seq):
    out = lax.map(block_fn, jnp.arange(seq // Q_BLOCK))
    nb, b, qb, h, dv = out.shape
    return out.transpose(1, 0, 2, 3, 4).reshape(b, seq, h * dv)


def _causal_softmax(s, start, v):
    seq = s.shape[-1]
    q_pos = start + jnp.arange(Q_BLOCK)
    allowed = jnp.arange(seq)[None, :] <= q_pos[:, None]
    p = jax.nn.softmax(jnp.where(allowed, s, -jnp.inf), axis=-1)
    return jnp.einsum('bhqs,bshd->bqhd', p.astype(v.dtype), v)


def fox_mixer(h, w_in, b_f, q_gain, k_gain, w_out):
    bsz, seq, _ = h.shape
    hd = N_HEADS * HEAD_DIM
    q, k, v, f_logit = jnp.split(h @ w_in, [hd, 2 * hd, 3 * hd], axis=-1)
    q = rms_norm(q.reshape(bsz, seq, N_HEADS, HEAD_DIM), q_gain)
    k = rms_norm(k.reshape(bsz, seq, N_HEADS, HEAD_DIM), k_gain)
    v = v.reshape(bsz, seq, N_HEADS, HEAD_DIM)
    log_f = jax.nn.log_sigmoid(f_logit.astype(jnp.float32) + b_f.astype(jnp.float32))
    cum = jnp.cumsum(log_f, axis=1).transpose(0, 2, 1)
    scale = HEAD_DIM ** -0.5

    def block(i):
        start = i * Q_BLOCK
        qb = lax.dynamic_slice_in_dim(q, start, Q_BLOCK, axis=1)
        cq = lax.dynamic_slice_in_dim(cum, start, Q_BLOCK, axis=2)
        s = jnp.einsum('bqhd,bshd->bhqs', qb, k, preferred_element_type=jnp.float32) * scale
        s = s + (cq[..., :, None] - cum[:, :, None, :])
        return _causal_softmax(s, start, v)

    return _sweep_query_blocks(block, seq) @ w_out


def _rope_tables(positions):
    inv_freq = ROPE_THETA ** (-jnp.arange(0, MLA_ROPE, 2, dtype=jnp.float32) / MLA_ROPE)
    ang = positions.astype(jnp.float32)[..., None] * inv_freq
    return jnp.cos(ang)[:, :, None, :], jnp.sin(ang)[:, :, None, :]


def _apply_rope(x, cos, sin):
    x1, x2 = jnp.split(x.astype(jnp.float32), 2, axis=-1)
    return jnp.concatenate([x1 * cos - x2 * sin, x1 * sin + x2 * cos], axis=-1).astype(x.dtype)


def mla_mixer(h, positions, w_in, q_a_gain, kv_a_gain, w_q_b, w_kv_b, q_gain, k_gain, w_out):
    bsz, seq, _ = h.shape
    c_q, c_kv, k_rope = jnp.split(h @ w_in, [MLA_Q_RANK, MLA_Q_RANK + MLA_KV_RANK], axis=-1)
    q = (rms_norm(c_q, q_a_gain) @ w_q_b).reshape(bsz, seq, N_HEADS, MLA_NOPE + MLA_ROPE)
    kv = (rms_norm(c_kv, kv_a_gain) @ w_kv_b).reshape(bsz, seq, N_HEADS, MLA_NOPE + MLA_V)
    q_nope, q_rope = jnp.split(q, [MLA_NOPE], axis=-1)
    k_nope, v = jnp.split(kv, [MLA_NOPE], axis=-1)
    cos, sin = _rope_tables(positions)
    q_nope = rms_norm(q_nope, q_gain[:MLA_NOPE])
    k_nope = rms_norm(k_nope, k_gain[:MLA_NOPE])
    q_rope = _apply_rope(rms_norm(q_rope, q_gain[MLA_NOPE:]), cos, sin)
    k_rope = _apply_rope(rms_norm(k_rope[:, :, None, :], k_gain[MLA_NOPE:]), cos, sin)
    q = jnp.concatenate([q_nope, q_rope], axis=-1)
    k = jnp.concatenate([k_nope, jnp.broadcast_to(k_rope, (bsz, seq, N_HEADS, MLA_ROPE))], axis=-1)
    scale = (MLA_NOPE + MLA_ROPE) ** -0.5

    def block(i):
        start = i * Q_BLOCK
        qb = lax.dynamic_slice_in_dim(q, start, Q_BLOCK, axis=1)
        s = jnp.einsum('bqhd,bshd->bhqs', qb, k, preferred_element_type=jnp.float32) * scale
        return _causal_softmax(s, start, v)

    return _sweep_query_blocks(block, seq) @ w_out


def stick_breaking_mixer(h, w_in, q_gain, k_gain, w_out):
    bsz, seq, _ = h.shape
    hd = N_HEADS * HEAD_DIM
    q, k, v = jnp.split(h @ w_in, [hd, 2 * hd], axis=-1)
    q = rms_norm(q.reshape(bsz, seq, N_HEADS, HEAD_DIM), q_gain)
    k = rms_norm(k.reshape(bsz, seq, N_HEADS, HEAD_DIM), k_gain)
    v = v.reshape(bsz, seq, N_HEADS, HEAD_DIM)
    scale = HEAD_DIM ** -0.5

    def block(i):
        start = i * Q_BLOCK
        qb = lax.dynamic_slice_in_dim(q, start, Q_BLOCK, axis=1)
        z = jnp.einsum('bqhd,bshd->bhqs', qb, k, preferred_element_type=jnp.float32) * scale
        q_pos = start + jnp.arange(Q_BLOCK)
        strict = jnp.arange(seq)[None, :] < q_pos[:, None]
        log_keep = jnp.where(strict, -jax.nn.softplus(z), 0.0)
        after = lax.cumsum(log_keep, axis=3, reverse=True) - log_keep
        a = jnp.where(strict, jnp.exp(jax.nn.log_sigmoid(z) + after), 0.0)
        return jnp.einsum('bhqs,bshd->bqhd', a.astype(v.dtype), v)

    return _sweep_query_blocks(block, seq) @ w_out


def sgu_mixer(h, w_in, v_gain, w_s, b_s, w_out):
    bsz, seq, _ = h.shape
    u, vv = jnp.split(jax.nn.gelu(h @ w_in), 2, axis=-1)
    vv = rms_norm(vv, v_gain)
    n_chunks = seq // SGU_CHUNK
    vv = vv.reshape(bsz, n_chunks, SGU_CHUNK, SGU_GROUPS, SGU_GROUP_DIM)
    causal = jnp.tril(jnp.ones((SGU_CHUNK, SGU_CHUNK), dtype=bool))
    ws = jnp.where(causal[None], w_s, 0.0).astype(vv.dtype)
    mixed = jnp.einsum('gts,bnsgc->bntgc', ws, vv) + b_s.T[:, :, None]
    return (u * mixed.reshape(bsz, seq, SGU_WIDTH)) @ w_out


def conv_ffn(h, w_up, conv_w, conv_b, w_down):
    seq = h.shape[1]
    up = h @ w_up
    padded = jnp.pad(up, ((0, 0), (CONV_WIDTH - 1, 0), (0, 0)))
    y = conv_b + conv_w[0] * padded[:, 0:seq]
    for tap in range(1, CONV_WIDTH):
        y = y + conv_w[tap] * padded[:, tap:tap + seq]
    gate, val = jnp.split(y, 2, axis=-1)
    return (jax.nn.silu(gate) * val) @ w_down


def setup_inputs(seed: int = 0) -> dict:
    key = jax.random.key(seed)
    ks = iter(list(jax.random.split(key, 32)))

    def w(shape, fan_in):
        return jax.random.normal(next(ks), shape, jnp.float32) * fan_in ** -0.5

    def gain(shape):
        return 1.0 + 0.1 * jax.random.normal(next(ks), shape, jnp.float32)

    n_a, n_b, n_c, n_d = (_n_layers_of(m) for m in range(N_MIXERS))
    hd = N_HEADS * HEAD_DIM
    x = jax.random.normal(next(ks), (BATCH, SEQ, D_MODEL), jnp.float32)
    offset = jax.random.randint(next(ks), (BATCH, 1), 0, 4096, dtype=jnp.int32)
    positions = offset + jnp.arange(SEQ, dtype=jnp.int32)[None, :]
    return {
        'x': x,
        'positions': positions,
        'mix_norm': gain((DEPTH, D_MODEL)),
        'ffn_norm': gain((DEPTH, D_MODEL)),
        'fox_w_in': w((n_a, D_MODEL, 3 * hd + N_HEADS), D_MODEL),
        'fox_b_f': FOX_FGATE_BIAS + 0.5 * jax.random.normal(next(ks), (n_a, N_HEADS), jnp.float32),
        'fox_q_gain': gain((n_a, HEAD_DIM)),
        'fox_k_gain': gain((n_a, HEAD_DIM)),
        'fox_w_out': w((n_a, hd, D_MODEL), hd),
        'mla_w_in': w((n_b, D_MODEL, MLA_Q_RANK + MLA_KV_RANK + MLA_ROPE), D_MODEL),
        'mla_q_a_gain': gain((n_b, MLA_Q_RANK)),
        'mla_kv_a_gain': gain((n_b, MLA_KV_RANK)),
        'mla_w_q_b': w((n_b, MLA_Q_RANK, N_HEADS * (MLA_NOPE + MLA_ROPE)), MLA_Q_RANK),
        'mla_w_kv_b': w((n_b, MLA_KV_RANK, N_HEADS * (MLA_NOPE + MLA_V)), MLA_KV_RANK),
        'mla_q_gain': gain((n_b, MLA_NOPE + MLA_ROPE)),
        'mla_k_gain': gain((n_b, MLA_NOPE + MLA_ROPE)),
        'mla_w_out': w((n_b, N_HEADS * MLA_V, D_MODEL), N_HEADS * MLA_V),
        'sb_w_in': w((n_c, D_MODEL, 3 * hd), D_MODEL),
        'sb_q_gain': gain((n_c, HEAD_DIM)),
        'sb_k_gain': gain((n_c, HEAD_DIM)),
        'sb_w_out': w((n_c, hd, D_MODEL), hd),
        'sgu_w_in': w((n_d, D_MODEL, 2 * SGU_WIDTH), D_MODEL),
        'sgu_v_gain': gain((n_d, SGU_WIDTH)),
        'sgu_w_s': w((n_d, SGU_GROUPS, SGU_CHUNK, SGU_CHUNK), SGU_CHUNK),
        'sgu_b_s': gain((n_d, SGU_GROUPS, SGU_CHUNK)),
        'sgu_w_out': w((n_d, SGU_WIDTH, D_MODEL), SGU_WIDTH),
        'ffn_w_up': w((DEPTH, D_MODEL, 2 * D_FF), D_MODEL),
        'ffn_conv_w': w((DEPTH, CONV_WIDTH, 2 * D_FF), CONV_WIDTH),
        'ffn_conv_b': 0.01 * jax.random.normal(next(ks), (DEPTH, 2 * D_FF), jnp.float32),
        'ffn_w_down': w((DEPTH, D_FF, D_MODEL), D_FF),
    }


def reference(x, positions, mix_norm, ffn_norm,
              fox_w_in, fox_b_f, fox_q_gain, fox_k_gain, fox_w_out,
              mla_w_in, mla_q_a_gain, mla_kv_a_gain, mla_w_q_b, mla_w_kv_b,
              mla_q_gain, mla_k_gain, mla_w_out,
              sb_w_in, sb_q_gain, sb_k_gain, sb_w_out,
              sgu_w_in, sgu_v_gain, sgu_w_s, sgu_b_s, sgu_w_out,
              ffn_w_up, ffn_conv_w, ffn_conv_b, ffn_w_down):
    h = x
    for i in range(DEPTH):
        m, j = i % N_MIXERS, i // N_MIXERS
        a = rms_norm(h, mix_norm[i])
        if m == 0:
            mixed = fox_mixer(a, fox_w_in[j], fox_b_f[j], fox_q_gain[j], fox_k_gain[j], fox_w_out[j])
        elif m == 1:
            mixed = mla_mixer(a, positions, mla_w_in[j], mla_q_a_gain[j], mla_kv_a_gain[j],
                              mla_w_q_b[j], mla_w_kv_b[j], mla_q_gain[j], mla_k_gain[j], mla_w_out[j])
        elif m == 2:
            mixed = stick_breaking_mixer(a, sb_w_in[j], sb_q_gain[j], sb_k_gain[j], sb_w_out[j])
        else:
            mixed = sgu_mixer(a, sgu_w_in[j], sgu_v_gain[j], sgu_w_s[j], sgu_b_s[j], sgu_w_out[j])
        h = h + mixed
        h = h + conv_ffn(rms_norm(h, ffn_norm[i]), ffn_w_up[i], ffn_conv_w[i], ffn_conv_b[i], ffn_w_down[i])
    return h
```

```python
import functools

import jax
import jax.numpy as jnp
from jax import lax
from jax.experimental import pallas as pl
from jax.experimental.pallas import tpu as pltpu

F32 = jnp.float32
BF16 = jnp.bfloat16

EPS = 1e-6
N_HEADS = 16
HEAD_DIM = 128
LANES = 128
SUBLANES = 8
MLA_Q_RANK = 512
MLA_KV_RANK = 512
MLA_NOPE = 128
MLA_ROPE = 64
ROPE_THETA = 10000.0
SGU_CHUNK = 128
SGU_GROUPS = 16
CONV_WIDTH = 3
VMEM_LIMIT_BYTES = 56 * 1024 * 1024


def _params(*sem):
    return pltpu.CompilerParams(dimension_semantics=sem, vmem_limit_bytes=VMEM_LIMIT_BYTES)


def _pick(total, pref):
    t = min(pref, total)
    while total % t:
        t //= 2
    return t


def _rms(x, gain):
    ms = jnp.mean(x * x, axis=-1, keepdims=True)
    return (x * lax.rsqrt(ms + EPS)) * gain


def _norm_proj_kernel(x_ref, g_ref, w_ref, *rest, epilogue, n_extra):
    extra = rest[:n_extra]
    o_ref = rest[n_extra]
    xn_ref = rest[n_extra + 1]
    j = pl.program_id(1)

    @pl.when(j == 0)
    def _():
        xn_ref[...] = _rms(x_ref[...], g_ref[...]).astype(BF16)

    acc = jnp.dot(xn_ref[...], w_ref[...], preferred_element_type=F32)
    epilogue(acc, j, o_ref, *extra)


def _norm_proj(x, gain, w, epilogue, out_dtype, *, k, x_col_block=0, tm=1024, tn=1024,
               extras=(), extra_specs=(), name):
    t = x.shape[0]
    n = w.shape[1]
    tm = _pick(t, tm)
    tn = _pick(n, tn)
    kern = functools.partial(_norm_proj_kernel, epilogue=epilogue, n_extra=len(extras))
    return pl.pallas_call(
        kern,
        grid=(t // tm, n // tn),
        in_specs=[
            pl.BlockSpec((tm, k), lambda i, j: (i, x_col_block)),
            pl.BlockSpec((1, k), lambda i, j: (0, 0)),
            pl.BlockSpec((k, tn), lambda i, j: (0, j)),
            *extra_specs,
        ],
        out_specs=pl.BlockSpec((tm, tn), lambda i, j: (i, j)),
        out_shape=jax.ShapeDtypeStruct((t, n), out_dtype),
        scratch_shapes=[pltpu.VMEM((tm, k), BF16)],
        compiler_params=_params("parallel", "arbitrary"),
        name=name,
    )(x, gain.reshape(1, k), w, *extras)


def _plain_epilogue(acc, j, o_ref):
    o_ref[...] = acc.astype(o_ref.dtype)


def _gelu_epilogue(acc, j, o_ref):
    c = 0.7978845608028654
    cdf = 0.5 * (1.0 + jnp.tanh(c * (acc + 0.044715 * (acc * acc * acc))))
    o_ref[...] = (acc * cdf).astype(o_ref.dtype)


def _headnorm_epilogue(acc, j, o_ref, hg_ref, *, n_norm_tiles):
    tn = acc.shape[1]

    @pl.when(j < n_norm_tiles)
    def _():
        g = hg_ref[...]
        for hh in range(tn // HEAD_DIM):
            sl = slice(hh * HEAD_DIM, (hh + 1) * HEAD_DIM)
            o_ref[:, sl] = _rms(acc[:, sl], g[:, sl]).astype(o_ref.dtype)

    @pl.when(j >= n_norm_tiles)
    def _():
        o_ref[...] = acc.astype(o_ref.dtype)


def _rope(x, cos_t, sin_t):
    lane = lax.broadcasted_iota(jnp.int32, x.shape, 1)
    half = MLA_ROPE // 2
    partner = jnp.where(lane < half, pltpu.roll(x, LANES - half, axis=1),
                        pltpu.roll(x, half, axis=1))
    return x * cos_t + partner * sin_t


def _rms_rope(x, gain, cos_t, sin_t):
    ms = jnp.sum(x * x, axis=-1, keepdims=True) * (1.0 / MLA_ROPE)
    return _rope((x * lax.rsqrt(ms + EPS)) * gain, cos_t, sin_t)


def _mla_q_epilogue(acc, j, o_ref, hg_ref, cos_ref, sin_ref, *, scale):
    tn = acc.shape[1]
    g = hg_ref[...]
    cos_t = cos_ref[...]
    sin_t = sin_ref[...]
    for hh in range(tn // (2 * LANES)):
        a = slice(hh * 2 * LANES, hh * 2 * LANES + LANES)
        b = slice(hh * 2 * LANES + LANES, (hh + 1) * 2 * LANES)
        o_ref[:, a] = (_rms(acc[:, a], g[:, a]) * scale).astype(o_ref.dtype)
        o_ref[:, b] = (_rms_rope(acc[:, b], g[:, b], cos_t, sin_t) * scale).astype(o_ref.dtype)


def _krope_kernel(y_ref, g_ref, cos_ref, sin_ref, o_ref):
    o_ref[...] = _rms_rope(y_ref[...], g_ref[...], cos_ref[...], sin_ref[...]).astype(o_ref.dtype)


def _rope_table_kernel(pos_ref, inv_ref, cmask_ref, ssign_ref, cos_ref, sin_ref):
    ang = pos_ref[...].astype(F32) * inv_ref[...]
    cos_ref[...] = jnp.cos(ang) * cmask_ref[...]
    sin_ref[...] = jnp.sin(ang) * ssign_ref[...]


def _proj_res_kernel(x_ref, w_ref, r_ref, o_ref):
    o_ref[...] = r_ref[...] + jnp.dot(x_ref[...], w_ref[...], preferred_element_type=F32)


def _proj_residual(x, w, res, *, tm=1024, tn=1024, name):
    t, k = x.shape
    n = w.shape[1]
    tm = _pick(t, tm)
    tn = _pick(n, tn)
    return pl.pallas_call(
        _proj_res_kernel,
        grid=(t // tm, n // tn),
        in_specs=[
            pl.BlockSpec((tm, k), lambda i, j: (i, 0)),
            pl.BlockSpec((k, tn), lambda i, j: (0, j)),
            pl.BlockSpec((tm, tn), lambda i, j: (i, j)),
        ],
        out_specs=pl.BlockSpec((tm, tn), lambda i, j: (i, j)),
        out_shape=jax.ShapeDtypeStruct((t, n), F32),
        compiler_params=_params("parallel", "parallel"),
        name=name,
    )(x, w, res)


def _ffn_kernel(h_ref, g_ref, wg_ref, wv_ref, cwg_ref, cwv_ref, cbg_ref, cbv_ref, wd_ref,
                o_ref, xn_ref, carry_ref, *, tiles_per_seq):
    i = pl.program_id(0)
    f = pl.program_id(1)
    tf = wg_ref.shape[1]

    @pl.when(f == 0)
    def _():
        xn_ref[...] = _rms(h_ref[...], g_ref[...]).astype(BF16)

    @pl.when(i % tiles_per_seq == 0)
    def _():
        carry_ref[f] = jnp.zeros(carry_ref.shape[1:], F32)

    xn = xn_ref[...]
    prev = carry_ref[f]

    def conv(u, prev_part, cw_ref, cb_ref):
        r1 = pltpu.roll(u, 1, axis=0)
        r2 = pltpu.roll(u, 2, axis=0)
        row = lax.broadcasted_iota(jnp.int32, u.shape, 0)
        u1 = jnp.where(row == 0, prev_part[1:2], r1)
        u2 = jnp.where(row == 0, prev_part[0:1], jnp.where(row == 1, prev_part[1:2], r2))
        y = cb_ref[...] + cw_ref[0:1] * u2
        y = y + cw_ref[1:2] * u1
        y = y + cw_ref[2:3] * u
        return y, r2[0:SUBLANES]

    ug = jnp.dot(xn, wg_ref[...], preferred_element_type=F32)
    uv = jnp.dot(xn, wv_ref[...], preferred_element_type=F32)
    gate, new_g = conv(ug, prev[:, :tf], cwg_ref, cbg_ref)
    val, new_v = conv(uv, prev[:, tf:], cwv_ref, cbv_ref)
    carry_ref[f, :, 0:tf] = new_g
    carry_ref[f, :, tf:2 * tf] = new_v

    act = ((gate * (1.0 / (1.0 + jnp.exp(-gate)))) * val).astype(BF16)
    d = jnp.dot(act, wd_ref[...], preferred_element_type=F32)

    @pl.when(f == 0)
    def _():
        o_ref[...] = h_ref[...] + d

    @pl.when(f > 0)
    def _():
        o_ref[...] += d


def _ffn(h, gain, w_up, conv_w, conv_b, w_down, *, seq, tm=512, tf=512):
    t, d = h.shape
    d_ff = w_down.shape[0]
    tm = _pick(seq, tm)
    tf = _pick(d_ff, tf)
    nf = d_ff // tf
    kern = functools.partial(_ffn_kernel, tiles_per_seq=seq // tm)
    return pl.pallas_call(
        kern,
        grid=(t // tm, nf),
        in_specs=[
            pl.BlockSpec((tm, d), lambda i, f: (i, 0)),
            pl.BlockSpec((1, d), lambda i, f: (0, 0)),
            pl.BlockSpec((d, tf), lambda i, f: (0, f)),
            pl.BlockSpec((d, tf), lambda i, f: (0, nf + f)),
            pl.BlockSpec((CONV_WIDTH, tf), lambda i, f: (0, f)),
            pl.BlockSpec((CONV_WIDTH, tf), lambda i, f: (0, nf + f)),
            pl.BlockSpec((1, tf), lambda i, f: (0, f)),
            pl.BlockSpec((1, tf), lambda i, f: (0, nf + f)),
            pl.BlockSpec((tf, d), lambda i, f: (f, 0)),
        ],
        out_specs=pl.BlockSpec((tm, d), lambda i, f: (i, 0)),
        out_shape=jax.ShapeDtypeStruct((t, d), F32),
        scratch_shapes=[pltpu.VMEM((tm, d), BF16), pltpu.VMEM((nf, SUBLANES, 2 * tf), F32)],
        compiler_params=_params("arbitrary", "arbitrary"),
        name="conv_ffn",
    )(h, gain.reshape(1, d), w_up, w_up, conv_w, conv_w, conv_b.reshape(1, -1),
      conv_b.reshape(1, -1), w_down)


def _fox_cum_kernel(f_ref, b_ref, o_ref, *, chunk):
    s = f_ref.shape[0]
    r = lax.broadcasted_iota(jnp.int32, (chunk, chunk), 0)
    c = lax.broadcasted_iota(jnp.int32, (chunk, chunk), 1)
    tri = (r <= c).astype(BF16)
    carry = jnp.zeros((LANES, 1), F32)
    for ci in range(s // chunk):
        x = f_ref[ci * chunk:(ci + 1) * chunk, :] + b_ref[...]
        lf = jnp.minimum(x, 0.0) - jnp.log1p(jnp.exp(-jnp.abs(x)))
        lft = lf.T
        hi = lft.astype(BF16)
        r1 = lft - hi.astype(F32)
        mid = r1.astype(BF16)
        lo = (r1 - mid.astype(F32)).astype(BF16)
        local = (jnp.dot(hi, tri, preferred_element_type=F32)
                 + jnp.dot(mid, tri, preferred_element_type=F32)
                 + jnp.dot(lo, tri, preferred_element_type=F32)) + carry
        o_ref[:, ci * chunk:(ci + 1) * chunk] = local[:N_HEADS]
        carry = local[:, chunk - 1:chunk]


def _fox_cum(f_logit, b_f, *, batch, seq):
    chunk = _pick(seq, 512)
    return pl.pallas_call(
        functools.partial(_fox_cum_kernel, chunk=chunk),
        grid=(batch,),
        in_specs=[pl.BlockSpec((seq, LANES), lambda b: (b, 0)),
                  pl.BlockSpec((1, LANES), lambda b: (0, 0))],
        out_specs=pl.BlockSpec((None, N_HEADS, seq), lambda b: (b, 0, 0)),
        out_shape=jax.ShapeDtypeStruct((batch, N_HEADS, seq), F32),
        compiler_params=_params("parallel"),
        name="fox_cum",
    )(f_logit, b_f)


def _softmax_attn_kernel(*refs, tq, mode):
    if mode == "fox":
        q_ref, k_ref, v_ref, c_ref, o_ref = refs
    else:
        q_ref, k_ref, kr_ref, v_ref, o_ref = refs
    h = pl.program_id(1)
    seq = q_ref.shape[0]
    row = lax.broadcasted_iota(jnp.int32, (tq, tq), 0)
    col = lax.broadcasted_iota(jnp.int32, (tq, tq), 1)
    allowed = col <= row

    for qi in range(seq // tq):
        q = q_ref[qi * tq:(qi + 1) * tq, :]

        def step(kj, carry, masked, q=q):
            m, l, acc = carry
            ks = pl.ds(pl.multiple_of(kj * tq, tq), tq)
            k = k_ref[ks, :]
            if mode == "mla":
                k = jnp.concatenate([k, kr_ref[ks, :]], axis=1)
            s = lax.dot_general(q, k, (((1,), (1,)), ((), ())), preferred_element_type=F32)
            if mode == "fox":
                s = s - c_ref[pl.ds(h, 1), ks]
            if masked:
                s = jnp.where(allowed, s, -jnp.inf)
            m_new = jnp.maximum(m, jnp.max(s, axis=1, keepdims=True))
            alpha = jnp.exp(m - m_new)
            p = jnp.exp(s - m_new)
            l = alpha * l + jnp.sum(p, axis=1, keepdims=True)
            acc = alpha * acc + jnp.dot(p.astype(BF16), v_ref[ks, :], preferred_element_type=F32)
            return m_new, l, acc

        carry = (jnp.full((tq, 1), -jnp.inf, F32), jnp.zeros((tq, 1), F32),
                 jnp.zeros((tq, HEAD_DIM), F32))
        if qi > 0:
            carry = lax.fori_loop(0, qi, lambda kj, c: step(kj, c, False), carry)
        _, l, acc = step(qi, carry, True)
        o_ref[qi * tq:(qi + 1) * tq, :] = (acc / l).astype(o_ref.dtype)


def _fox_attention(qkv, cum, *, batch, seq, tq=512):
    t = qkv.shape[0]
    tq = _pick(seq, tq)
    blk = lambda off: pl.BlockSpec((seq, HEAD_DIM), lambda b, h: (b, off + h))
    return pl.pallas_call(
        functools.partial(_softmax_attn_kernel, tq=tq, mode="fox"),
        grid=(batch, N_HEADS),
        in_specs=[blk(0), blk(N_HEADS), blk(2 * N_HEADS),
                  pl.BlockSpec((None, N_HEADS, seq), lambda b, h: (b, 0, 0))],
        out_specs=blk(0),
        out_shape=jax.ShapeDtypeStruct((t, N_HEADS * HEAD_DIM), BF16),
        compiler_params=_params("parallel", "parallel"),
        name="fox_attention",
    )(qkv, qkv, qkv, cum)


def _mla_attention(q, kv, k_rope, *, batch, seq, tq=512):
    t = q.shape[0]
    tq = _pick(seq, tq)
    return pl.pallas_call(
        functools.partial(_softmax_attn_kernel, tq=tq, mode="mla"),
        grid=(batch, N_HEADS),
        in_specs=[pl.BlockSpec((seq, 2 * LANES), lambda b, h: (b, h)),
                  pl.BlockSpec((seq, HEAD_DIM), lambda b, h: (b, h)),
                  pl.BlockSpec((seq, LANES), lambda b, h: (b, 0)),
                  pl.BlockSpec((seq, HEAD_DIM), lambda b, h: (b, N_HEADS + h))],
        out_specs=pl.BlockSpec((seq, HEAD_DIM), lambda b, h: (b, h)),
        out_shape=jax.ShapeDtypeStruct((t, N_HEADS * HEAD_DIM), BF16),
        compiler_params=_params("parallel", "parallel"),
        name="mla_attention",
    )(q, kv, k_rope, kv)


def _sb_attn_kernel(q_ref, k_ref, v_ref, o_ref, *, tq):
    seq = q_ref.shape[0]
    row = lax.broadcasted_iota(jnp.int32, (tq, tq), 0)
    col = lax.broadcasted_iota(jnp.int32, (tq, tq), 1)
    strict = col < row
    suffix = (row >= col).astype(BF16)

    for qi in range(seq // tq):
        q = q_ref[qi * tq:(qi + 1) * tq, :]

        def step(kj, carry, masked, q=q):
            acc, later = carry
            ks = pl.ds(pl.multiple_of(kj * tq, tq), tq)
            z = lax.dot_general(q, k_ref[ks, :], (((1,), (1,)), ((), ())),
                                preferred_element_type=F32)
            t = jnp.log1p(jnp.exp(-jnp.abs(z)))
            log_keep = -jnp.maximum(z, 0.0) - t
            log_beta = jnp.minimum(z, 0.0) - t
            if masked:
                log_keep = jnp.where(strict, log_keep, 0.0)
            hi = log_keep.astype(BF16)
            lo = (log_keep - hi.astype(F32)).astype(BF16)
            incl = (jnp.dot(hi, suffix, preferred_element_type=F32)
                    + jnp.dot(lo, suffix, preferred_element_type=F32))
            a = jnp.exp(log_beta + ((incl - log_keep) + later))
            if masked:
                a = jnp.where(strict, a, 0.0)
            acc = acc + jnp.dot(a.astype(BF16), v_ref[ks, :], preferred_element_type=F32)
            return acc, later + incl[:, 0:1]

        carry = step(qi, (jnp.zeros((tq, HEAD_DIM), F32), jnp.zeros((tq, 1), F32)), True)
        if qi > 0:
            carry = lax.fori_loop(0, qi, lambda it, c: step(qi - 1 - it, c, False), carry)
        o_ref[qi * tq:(qi + 1) * tq, :] = carry[0].astype(o_ref.dtype)


def _sb_attention(qkv, *, batch, seq, tq=256):
    t = qkv.shape[0]
    tq = _pick(seq, tq)
    blk = lambda off: pl.BlockSpec((seq, HEAD_DIM), lambda b, h: (b, off + h))
    return pl.pallas_call(
        functools.partial(_sb_attn_kernel, tq=tq),
        grid=(batch, N_HEADS),
        in_specs=[blk(0), blk(N_HEADS), blk(2 * N_HEADS)],
        out_specs=blk(0),
        out_shape=jax.ShapeDtypeStruct((t, N_HEADS * HEAD_DIM), BF16),
        compiler_params=_params("parallel", "parallel"),
        name="sb_attention",
    )(qkv, qkv, qkv)


def _sgu_kernel(u_ref, v_ref, g_ref, ws_ref, bs_ref, o_ref):
    tm, width = v_ref.shape
    row = lax.broadcasted_iota(jnp.int32, (SGU_CHUNK, SGU_CHUNK), 0)
    col = lax.broadcasted_iota(jnp.int32, (SGU_CHUNK, SGU_CHUNK), 1)
    causal = col <= row
    vn = _rms(v_ref[...].astype(F32), g_ref[...]).astype(BF16)
    bs = bs_ref[...]
    for g in range(width // LANES):
        cs = slice(g * LANES, (g + 1) * LANES)
        w = jnp.where(causal, ws_ref[g], 0.0).astype(BF16)
        for n in range(tm // SGU_CHUNK):
            rs = slice(n * SGU_CHUNK, (n + 1) * SGU_CHUNK)
            mixed = jnp.dot(w, vn[rs, cs], preferred_element_type=F32) + bs[:, g:g + 1]
            o_ref[rs, cs] = (u_ref[rs, cs].astype(F32) * mixed).astype(o_ref.dtype)


def _sgu_spatial(uv, v_gain, w_s, b_s, *, tm=512):
    t = uv.shape[0]
    width = uv.shape[1] // 2
    tm = _pick(t, tm)
    return pl.pallas_call(
        _sgu_kernel,
        grid=(t // tm,),
        in_specs=[pl.BlockSpec((tm, width), lambda i: (i, 0)),
                  pl.BlockSpec((tm, width), lambda i: (i, 1)),
                  pl.BlockSpec((1, width), lambda i: (0, 0)),
                  pl.BlockSpec(w_s.shape, lambda i: (0, 0, 0)),
                  pl.BlockSpec((SGU_CHUNK, SGU_GROUPS), lambda i: (0, 0))],
        out_specs=pl.BlockSpec((tm, width), lambda i: (i, 0)),
        out_shape=jax.ShapeDtypeStruct((t, width), BF16),
        compiler_params=_params("parallel"),
        name="sgu_spatial",
    )(uv, uv, v_gain.reshape(1, width), w_s, b_s.T)


def _tile_gain(gain, reps):
    return jnp.tile(gain.astype(F32), reps)


def _qkv_mixer_proj(h, norm_gain, w_qkv, q_gain, k_gain, scale, name):
    hd = N_HEADS * HEAD_DIM
    tn = 1024
    per = tn // HEAD_DIM
    gains = jnp.stack([_tile_gain(q_gain * scale, per), _tile_gain(k_gain, per),
                       jnp.ones((tn,), F32)]).reshape(3, 1, tn)
    tiles_per_seg = hd // tn
    epi = functools.partial(_headnorm_epilogue, n_norm_tiles=2 * tiles_per_seg)
    return _norm_proj(
        h, norm_gain, w_qkv.astype(BF16), epi, BF16, k=h.shape[1], tn=tn,
        extras=(gains,),
        extra_specs=(pl.BlockSpec((None, 1, tn), lambda i, j: (j // tiles_per_seg, 0, 0)),),
        name=name)


def _fox_mixer(h, norm_gain, w_in, b_f, q_gain, k_gain, w_out, *, batch, seq):
    hd = N_HEADS * HEAD_DIM
    d = h.shape[1]
    qkv = _qkv_mixer_proj(h, norm_gain, w_in[:, :3 * hd], q_gain, k_gain, HEAD_DIM ** -0.5,
                          "fox_qkv_proj")
    w_f = jnp.pad(w_in[:, 3 * hd:], ((0, 0), (0, LANES - N_HEADS))).astype(BF16)
    f_logit = _norm_proj(h, norm_gain, w_f, _plain_epilogue, F32, k=d, tn=LANES,
                         name="fox_fgate_proj")
    b_pad = jnp.pad(b_f.astype(F32), (0, LANES - N_HEADS)).reshape(1, LANES)
    cum = _fox_cum(f_logit, b_pad, batch=batch, seq=seq)
    o = _fox_attention(qkv, cum, batch=batch, seq=seq)
    return _proj_residual(o, w_out.astype(BF16), h, name="fox_out_proj")


def _mla_mixer(h, positions, norm_gain, w_in, q_a_gain, kv_a_gain, w_q_b, w_kv_b, q_gain,
               k_gain, w_out, *, batch, seq):
    t, d = h.shape
    n_in = MLA_Q_RANK + MLA_KV_RANK + MLA_ROPE
    n_in_pad = MLA_Q_RANK + MLA_KV_RANK + LANES
    w_in_p = jnp.pad(w_in, ((0, 0), (0, n_in_pad - n_in))).astype(BF16)
    y = _norm_proj(h, norm_gain, w_in_p, _plain_epilogue, F32, k=d, tn=n_in_pad,
                   name="mla_down_proj")

    half = MLA_ROPE // 2
    inv_freq = ROPE_THETA ** (-jnp.arange(0, MLA_ROPE, 2, dtype=F32) / MLA_ROPE)
    zeros = jnp.zeros((LANES - MLA_ROPE,), F32)
    inv_row = jnp.concatenate([inv_freq, inv_freq, zeros]).reshape(1, LANES)
    cmask = jnp.concatenate([jnp.ones((MLA_ROPE,), F32), zeros]).reshape(1, LANES)
    ssign = jnp.concatenate([-jnp.ones((half,), F32), jnp.ones((half,), F32), zeros]).reshape(1, LANES)
    tm_small = _pick(t, 1024)
    row_spec = pl.BlockSpec((tm_small, LANES), lambda i: (i, 0))
    const_spec = pl.BlockSpec((1, LANES), lambda i: (0, 0))
    cos_t, sin_t = pl.pallas_call(
        _rope_table_kernel,
        grid=(t // tm_small,),
        in_specs=[pl.BlockSpec((tm_small, 1), lambda i: (i, 0)), const_spec, const_spec, const_spec],
        out_specs=[row_spec, row_spec],
        out_shape=[jax.ShapeDtypeStruct((t, LANES), F32)] * 2,
        compiler_params=_params("parallel"),
        name="rope_tables",
    )(positions.reshape(t, 1), inv_row, cmask, ssign)

    scale = (MLA_NOPE + MLA_ROPE) ** -0.5
    wq = w_q_b.reshape(MLA_Q_RANK, N_HEADS, MLA_NOPE + MLA_ROPE)
    wq = jnp.pad(wq, ((0, 0), (0, 0), (0, 2 * LANES - MLA_NOPE - MLA_ROPE)))
    wq = wq.reshape(MLA_Q_RANK, N_HEADS * 2 * LANES).astype(BF16)
    tn = 1024
    q_head_gain = jnp.concatenate([q_gain.astype(F32), zeros])
    q_gains = _tile_gain(q_head_gain, tn // (2 * LANES)).reshape(1, tn)
    tm = _pick(t, 1024)
    tab_spec = pl.BlockSpec((tm, LANES), lambda i, j: (i, 0))
    q = _norm_proj(y, q_a_gain, wq, functools.partial(_mla_q_epilogue, scale=scale), BF16,
                   k=MLA_Q_RANK, x_col_block=0, tm=tm, tn=tn,
                   extras=(q_gains, cos_t, sin_t),
                   extra_specs=(pl.BlockSpec((1, tn), lambda i, j: (0, 0)), tab_spec, tab_spec),
                   name="mla_q_proj")

    wkv = w_kv_b.reshape(MLA_KV_RANK, N_HEADS, MLA_NOPE + HEAD_DIM)
    wkv = jnp.concatenate([wkv[:, :, :MLA_NOPE].reshape(MLA_KV_RANK, -1),
                           wkv[:, :, MLA_NOPE:].reshape(MLA_KV_RANK, -1)], axis=1).astype(BF16)
    k_gains = jnp.stack([_tile_gain(k_gain[:MLA_NOPE], tn // HEAD_DIM),
                         jnp.ones((tn,), F32)]).reshape(2, 1, tn)
    tiles_per_seg = N_HEADS * HEAD_DIM // tn
    kv = _norm_proj(y, kv_a_gain, wkv,
                    functools.partial(_headnorm_epilogue, n_norm_tiles=tiles_per_seg), BF16,
                    k=MLA_KV_RANK, x_col_block=1, tn=tn,
                    extras=(k_gains,),
                    extra_specs=(pl.BlockSpec((None, 1, tn), lambda i, j: (j // tiles_per_seg, 0, 0)),),
                    name="mla_kv_proj")

    kr_gain = jnp.concatenate([k_gain[MLA_NOPE:].astype(F32), zeros]).reshape(1, LANES)
    k_rope = pl.pallas_call(
        _krope_kernel,
        grid=(t // tm_small,),
        in_specs=[pl.BlockSpec((tm_small, LANES), lambda i: (i, (MLA_Q_RANK + MLA_KV_RANK) // LANES)),
                  const_spec, row_spec, row_spec],
        out_specs=row_spec,
        out_shape=jax.ShapeDtypeStruct((t, LANES), BF16),
        compiler_params=_params("parallel"),
        name="mla_k_rope",
    )(y, kr_gain, cos_t, sin_t)

    o = _mla_attention(q, kv, k_rope, batch=batch, seq=seq)
    return _proj_residual(o, w_out.astype(BF16), h, name="mla_out_proj")


def _sb_mixer(h, norm_gain, w_in, q_gain, k_gain, w_out, *, batch, seq):
    qkv = _qkv_mixer_proj(h, norm_gain, w_in, q_gain, k_gain, HEAD_DIM ** -0.5, "sb_qkv_proj")
    o = _sb_attention(qkv, batch=batch, seq=seq)
    return _proj_residual(o, w_out.astype(BF16), h, name="sb_out_proj")


def _sgu_mixer(h, norm_gain, w_in, v_gain, w_s, b_s, w_out):
    uv = _norm_proj(h, norm_gain, w_in.astype(BF16), _gelu_epilogue, BF16, k=h.shape[1],
                    name="sgu_in_proj")
    gated = _sgu_spatial(uv, v_gain, w_s, b_s)
    return _proj_residual(gated, w_out.astype(BF16), h, name="sgu_out_proj")


def kernel(x, positions, mix_norm, ffn_norm, fox_w_in, fox_b_f, fox_q_gain, fox_k_gain, fox_w_out, mla_w_in, mla_q_a_gain, mla_kv_a_gain, mla_w_q_b, mla_w_kv_b, mla_q_gain, mla_k_gain, mla_w_out, sb_w_in, sb_q_gain, sb_k_gain, sb_w_out, sgu_w_in, sgu_v_gain, sgu_w_s, sgu_b_s, sgu_w_out, ffn_w_up, ffn_conv_w, ffn_conv_b, ffn_w_down):
    batch, seq, d = x.shape
    depth = mix_norm.shape[0]
    n_mixers = 4
    h = x.reshape(batch * seq, d)
    for i in range(depth):
        m, j = i % n_mixers, i // n_mixers
        if m == 0:
            h = _fox_mixer(h, mix_norm[i], fox_w_in[j], fox_b_f[j], fox_q_gain[j], fox_k_gain[j],
                           fox_w_out[j], batch=batch, seq=seq)
        elif m == 1:
            h = _mla_mixer(h, positions, mix_norm[i], mla_w_in[j], mla_q_a_gain[j],
                           mla_kv_a_gain[j], mla_w_q_b[j], mla_w_kv_b[j], mla_q_gain[j],
                           mla_k_gain[j], mla_w_out[j], batch=batch, seq=seq)
        elif m == 2:
            h = _sb_mixer(h, mix_norm[i], sb_w_in[j], sb_q_gain[j], sb_k_gain[j], sb_w_out[j],
                          batch=batch, seq=seq)
        else:
            h = _sgu_mixer(h, mix_norm[i], sgu_w_in[j], sgu_v_gain[j], sgu_w_s[j], sgu_b_s[j],
                           sgu_w_out[j])
        h = _ffn(h, ffn_norm[i], ffn_w_up[i].astype(BF16), ffn_conv_w[i], ffn_conv_b[i],
                 ffn_w_down[i].astype(BF16), seq=seq)
    return h.reshape(batch, seq, d)
```

```python
import functools

import jax
import jax.numpy as jnp
from jax import lax
from jax.experimental import pallas as pl
from jax.experimental.pallas import tpu as pltpu

F32 = jnp.float32
BF16 = jnp.bfloat16

EPS = 1e-6
N_HEADS = 16
HEAD_DIM = 128
LANES = 128
SUBLANES = 8
MLA_Q_RANK = 512
MLA_KV_RANK = 512
MLA_NOPE = 128
MLA_ROPE = 64
ROPE_THETA = 10000.0
SGU_CHUNK = 128
SGU_GROUPS = 16
CONV_WIDTH = 3
LOG2_E = 1.4426950408889634
VMEM_LIMIT_BYTES = 56 * 1024 * 1024


def _params(*sem, flags=None):
    return pltpu.CompilerParams(dimension_semantics=sem, vmem_limit_bytes=VMEM_LIMIT_BYTES,
                                flags=flags)


def _pick(total, pref):
    t = min(pref, total)
    while total % t:
        t //= 2
    return t


def _rms(x, gain):
    ms = jnp.mean(x * x, axis=-1, keepdims=True)
    return (x * lax.rsqrt(ms + EPS)) * gain


def _norm_proj_kernel(x_ref, g_ref, w_ref, *rest, epilogue, n_extra):
    extra = rest[:n_extra]
    o_ref = rest[n_extra]
    xn_ref = rest[n_extra + 1]
    j = pl.program_id(1)

    @pl.when(j == 0)
    def _():
        xn_ref[...] = _rms(x_ref[...], g_ref[...]).astype(BF16)

    acc = jnp.dot(xn_ref[...], w_ref[...], preferred_element_type=F32)
    epilogue(acc, j, o_ref, *extra)


def _norm_proj(x, gain, w, epilogue, out_dtype, *, k, x_col_block=0, tm=1024, tn=1024,
               extras=(), extra_specs=(), name):
    t = x.shape[0]
    n = w.shape[1]
    tm = _pick(t, tm)
    tn = _pick(n, tn)
    kern = functools.partial(_norm_proj_kernel, epilogue=epilogue, n_extra=len(extras))
    return pl.pallas_call(
        kern,
        grid=(t // tm, n // tn),
        in_specs=[
            pl.BlockSpec((tm, k), lambda i, j: (i, x_col_block)),
            pl.BlockSpec((1, k), lambda i, j: (0, 0)),
            pl.BlockSpec((k, tn), lambda i, j: (0, j)),
            *extra_specs,
        ],
        out_specs=pl.BlockSpec((tm, tn), lambda i, j: (i, j)),
        out_shape=jax.ShapeDtypeStruct((t, n), out_dtype),
        scratch_shapes=[pltpu.VMEM((tm, k), BF16)],
        compiler_params=_params("parallel", "arbitrary"),
        name=name,
    )(x, gain.reshape(1, k), w, *extras)


def _plain_epilogue(acc, j, o_ref):
    o_ref[...] = acc.astype(o_ref.dtype)


def _gelu_epilogue(acc, j, o_ref):
    c = 0.7978845608028654
    cdf = 0.5 * (1.0 + jnp.tanh(c * (acc + 0.044715 * (acc * acc * acc))))
    o_ref[...] = (acc * cdf).astype(o_ref.dtype)


def _headnorm_epilogue(acc, j, o_ref, hg_ref, *, n_norm_tiles):
    tn = acc.shape[1]

    @pl.when(j < n_norm_tiles)
    def _():
        g = hg_ref[...]
        for hh in range(tn // HEAD_DIM):
            sl = slice(hh * HEAD_DIM, (hh + 1) * HEAD_DIM)
            o_ref[:, sl] = _rms(acc[:, sl], g[:, sl]).astype(o_ref.dtype)

    @pl.when(j >= n_norm_tiles)
    def _():
        o_ref[...] = acc.astype(o_ref.dtype)


def _rope(x, cos_t, sin_t):
    lane = lax.broadcasted_iota(jnp.int32, x.shape, 1)
    half = MLA_ROPE // 2
    partner = jnp.where(lane < half, pltpu.roll(x, LANES - half, axis=1),
                        pltpu.roll(x, half, axis=1))
    return x * cos_t + partner * sin_t


def _rms_rope(x, gain, cos_t, sin_t):
    ms = jnp.sum(x * x, axis=-1, keepdims=True) * (1.0 / MLA_ROPE)
    return _rope((x * lax.rsqrt(ms + EPS)) * gain, cos_t, sin_t)


def _mla_q_epilogue(acc, j, o_ref, hg_ref, cos_ref, sin_ref, *, scale):
    tn = acc.shape[1]
    g = hg_ref[...]
    cos_t = cos_ref[...]
    sin_t = sin_ref[...]
    for hh in range(tn // (2 * LANES)):
        a = slice(hh * 2 * LANES, hh * 2 * LANES + LANES)
        b = slice(hh * 2 * LANES + LANES, (hh + 1) * 2 * LANES)
        o_ref[:, a] = (_rms(acc[:, a], g[:, a]) * scale).astype(o_ref.dtype)
        o_ref[:, b] = (_rms_rope(acc[:, b], g[:, b], cos_t, sin_t) * scale).astype(o_ref.dtype)


def _krope_kernel(y_ref, g_ref, cos_ref, sin_ref, o_ref):
    o_ref[...] = _rms_rope(y_ref[...], g_ref[...], cos_ref[...], sin_ref[...]).astype(o_ref.dtype)


def _rope_table_kernel(pos_ref, inv_ref, cmask_ref, ssign_ref, cos_ref, sin_ref):
    ang = pos_ref[...].astype(F32) * inv_ref[...]
    cos_ref[...] = jnp.cos(ang) * cmask_ref[...]
    sin_ref[...] = jnp.sin(ang) * ssign_ref[...]


def _proj_res_kernel(x_ref, w_ref, r_ref, o_ref):
    o_ref[...] = r_ref[...] + jnp.dot(x_ref[...], w_ref[...], preferred_element_type=F32)


def _proj_residual(x, w, res, *, tm=1024, tn=1024, name):
    t, k = x.shape
    n = w.shape[1]
    tm = _pick(t, tm)
    tn = _pick(n, tn)
    return pl.pallas_call(
        _proj_res_kernel,
        grid=(t // tm, n // tn),
        in_specs=[
            pl.BlockSpec((tm, k), lambda i, j: (i, 0)),
            pl.BlockSpec((k, tn), lambda i, j: (0, j)),
            pl.BlockSpec((tm, tn), lambda i, j: (i, j)),
        ],
        out_specs=pl.BlockSpec((tm, tn), lambda i, j: (i, j)),
        out_shape=jax.ShapeDtypeStruct((t, n), F32),
        compiler_params=_params("parallel", "parallel"),
        name=name,
    )(x, w, res)


def _ffn_jobs(s, n_jobs, nf):
    job_a = jnp.minimum(s, n_jobs - 1)
    job_b = jnp.maximum(s - 1, 0)
    return job_a // nf, job_a % nf, job_b // nf, job_b % nf


def _ffn_kernel(h_ref, g_ref, wg_ref, wv_ref, cwg_ref, cwv_ref, cbg_ref, cbv_ref, wd_ref,
                o_ref, xn_ref, u0_ref, u1_ref, act_ref, carry_ref, *, nf, n_jobs, tiles_per_seq):
    s = pl.program_id(0)
    tm = o_ref.shape[0]
    tf = wg_ref.shape[1]
    _, ca, tb, cb = _ffn_jobs(s, n_jobs, nf)
    pad = SUBLANES
    sub = min(tf, 2 * LANES)

    @pl.when(s == 0)
    def _():
        u1_ref[...] = jnp.zeros(u1_ref.shape, F32)

    @pl.when(ca == 0)
    def _():
        xn_ref[...] = _rms(h_ref[...], g_ref[...]).astype(BF16)

    @pl.when(cb == 0)
    def _():
        o_ref[...] = h_ref[...]

    @pl.when(tb % tiles_per_seq == 0)
    def _():
        carry_ref[cb] = jnp.zeros(carry_ref.shape[1:], F32)

    def conv(u_ref, cols, taps, bias):
        y = bias + taps[0:1] * u_ref[pad - 2:pad - 2 + tm, cols]
        y = y + taps[1:2] * u_ref[pad - 1:pad - 1 + tm, cols]
        y = y + taps[2:3] * u_ref[pad:pad + tm, cols]
        return y

    def step(u_write, u_read):
        xn = xn_ref[...]
        u_write[pad:pad + tm, 0:tf] = jnp.dot(xn, wg_ref[...], preferred_element_type=F32)
        u_write[pad:pad + tm, tf:2 * tf] = jnp.dot(xn, wv_ref[...], preferred_element_type=F32)
        u_read[0:pad, :] = carry_ref[cb]
        for c in range(tf // sub):
            gs = slice(c * sub, (c + 1) * sub)
            vs = slice(tf + c * sub, tf + (c + 1) * sub)
            gate = conv(u_read, gs, cwg_ref[:, gs], cbg_ref[:, gs])
            val = conv(u_read, vs, cwv_ref[:, gs], cbv_ref[:, gs])
            act_ref[:, gs] = ((gate * (1.0 / (1.0 + jnp.exp(-gate)))) * val).astype(BF16)
            o_ref[...] += jnp.dot(act_ref[:, gs], wd_ref[gs, :], preferred_element_type=F32)
        carry_ref[cb] = u_read[tm:tm + pad, :]

    @pl.when(s % 2 == 0)
    def _():
        step(u0_ref, u1_ref)

    @pl.when(s % 2 == 1)
    def _():
        step(u1_ref, u0_ref)


def _ffn(h, gain, w_up, conv_w, conv_b, w_down, *, seq, tm=512, tf=512):
    t, d = h.shape
    d_ff = w_down.shape[0]
    tm = _pick(seq, tm)
    tf = _pick(d_ff, tf)
    nf = d_ff // tf
    assert nf >= 2, "the output tile is initialised from the h block one step into a tile"
    n_jobs = (t // tm) * nf
    kern = functools.partial(_ffn_kernel, nf=nf, n_jobs=n_jobs, tiles_per_seq=seq // tm)
    jobs = functools.partial(_ffn_jobs, n_jobs=n_jobs, nf=nf)
    u_shape = (SUBLANES + tm, 2 * tf)
    return pl.pallas_call(
        kern,
        grid=(n_jobs + 1,),
        in_specs=[
            pl.BlockSpec((tm, d), lambda s: (jobs(s)[0], 0)),
            pl.BlockSpec((1, d), lambda s: (0, 0)),
            pl.BlockSpec((d, tf), lambda s: (0, jobs(s)[1])),
            pl.BlockSpec((d, tf), lambda s: (0, nf + jobs(s)[1])),
            pl.BlockSpec((CONV_WIDTH, tf), lambda s: (0, jobs(s)[3])),
            pl.BlockSpec((CONV_WIDTH, tf), lambda s: (0, nf + jobs(s)[3])),
            pl.BlockSpec((1, tf), lambda s: (0, jobs(s)[3])),
            pl.BlockSpec((1, tf), lambda s: (0, nf + jobs(s)[3])),
            pl.BlockSpec((tf, d), lambda s: (jobs(s)[3], 0)),
        ],
        out_specs=pl.BlockSpec((tm, d), lambda s: (jobs(s)[2], 0)),
        out_shape=jax.ShapeDtypeStruct((t, d), F32),
        scratch_shapes=[pltpu.VMEM((tm, d), BF16), pltpu.VMEM(u_shape, F32),
                        pltpu.VMEM(u_shape, F32), pltpu.VMEM((tm, tf), BF16),
                        pltpu.VMEM((nf, SUBLANES, 2 * tf), F32)],
        compiler_params=_params("arbitrary"),
        name="conv_ffn",
    )(h, gain.reshape(1, d), w_up, w_up, conv_w, conv_w, conv_b.reshape(1, -1),
      conv_b.reshape(1, -1), w_down)


def _fox_cum_kernel(f_ref, b_ref, o_ref, *, chunk):
    s = f_ref.shape[0]
    r = lax.broadcasted_iota(jnp.int32, (chunk, chunk), 0)
    c = lax.broadcasted_iota(jnp.int32, (chunk, chunk), 1)
    tri = (r <= c).astype(BF16)
    carry = jnp.zeros((LANES, 1), F32)
    for ci in range(s // chunk):
        x = f_ref[ci * chunk:(ci + 1) * chunk, :] + b_ref[...]
        lf = jnp.minimum(x, 0.0) - jnp.log1p(jnp.exp(-jnp.abs(x)))
        lft = lf.T
        hi = lft.astype(BF16)
        r1 = lft - hi.astype(F32)
        mid = r1.astype(BF16)
        lo = (r1 - mid.astype(F32)).astype(BF16)
        local = (jnp.dot(hi, tri, preferred_element_type=F32)
                 + jnp.dot(mid, tri, preferred_element_type=F32)
                 + jnp.dot(lo, tri, preferred_element_type=F32)) + carry
        o_ref[:, ci * chunk:(ci + 1) * chunk] = local[:N_HEADS]
        carry = local[:, chunk - 1:chunk]


def _fox_cum(f_logit, b_f, *, batch, seq):
    chunk = _pick(seq, 512)
    return pl.pallas_call(
        functools.partial(_fox_cum_kernel, chunk=chunk),
        grid=(batch,),
        in_specs=[pl.BlockSpec((seq, LANES), lambda b: (b, 0)),
                  pl.BlockSpec((1, LANES), lambda b: (0, 0))],
        out_specs=pl.BlockSpec((None, N_HEADS, seq), lambda b: (b, 0, 0)),
        out_shape=jax.ShapeDtypeStruct((batch, N_HEADS, seq), F32),
        compiler_params=_params("parallel"),
        name="fox_cum",
    )(f_logit, b_f)


def _softmax_attn_kernel(*refs, tq, mode):
    if mode == "fox":
        q_ref, k_ref, v_ref, c_ref, o_ref = refs
    else:
        q_ref, k_ref, kr_ref, v_ref, o_ref = refs
    h = pl.program_id(1)
    seq = q_ref.shape[0]
    row = lax.broadcasted_iota(jnp.int32, (tq, tq), 0)
    col = lax.broadcasted_iota(jnp.int32, (tq, tq), 1)
    allowed = col <= row

    def blk(i):
        return slice(i * tq, (i + 1) * tq)

    tiles = [(qi, kj) for qi in range(seq // tq) for kj in range(qi + 1)]
    scores = {}
    m = l = acc = None
    for n in range(len(tiles) + 1):
        if n < len(tiles):
            qi, kj = tiles[n]
            k = k_ref[blk(kj), :]
            if mode == "mla":
                k = jnp.concatenate([k, kr_ref[blk(kj), :]], axis=1)
            scores[n] = lax.dot_general(q_ref[blk(qi), :], k, (((1,), (1,)), ((), ())),
                                        preferred_element_type=F32)
        if n >= 1:
            qi, kj = tiles[n - 1]
            s = scores.pop(n - 1)
            if kj == 0:
                m = jnp.full((tq, 1), -jnp.inf, F32)
                l = jnp.zeros((tq, 1), F32)
                acc = jnp.zeros((tq, HEAD_DIM), F32)
            if mode == "fox":
                s = s - c_ref[pl.ds(h, 1), blk(kj)]
            if kj == qi:
                s = jnp.where(allowed, s, -jnp.inf)
            m_new = jnp.maximum(m, jnp.max(s, axis=1, keepdims=True))
            alpha = jnp.exp(m - m_new)
            p = jnp.exp(s - m_new)
            l = alpha * l + jnp.sum(p, axis=1, keepdims=True)
            acc = alpha * acc + jnp.dot(p.astype(BF16), v_ref[blk(kj), :],
                                        preferred_element_type=F32)
            m = m_new
            if kj == qi:
                o_ref[blk(qi), :] = (acc / l).astype(o_ref.dtype)


def _fox_attention(qkv, cum, *, batch, seq, tq=512):
    t = qkv.shape[0]
    tq = _pick(seq, tq)
    blk = lambda off: pl.BlockSpec((seq, HEAD_DIM), lambda b, h: (b, off + h))
    return pl.pallas_call(
        functools.partial(_softmax_attn_kernel, tq=tq, mode="fox"),
        grid=(batch, N_HEADS),
        in_specs=[blk(0), blk(N_HEADS), blk(2 * N_HEADS),
                  pl.BlockSpec((None, N_HEADS, seq), lambda b, h: (b, 0, 0))],
        out_specs=blk(0),
        out_shape=jax.ShapeDtypeStruct((t, N_HEADS * HEAD_DIM), BF16),
        compiler_params=_params("parallel", "parallel"),
        name="fox_attention",
    )(qkv, qkv, qkv, cum)


def _mla_attention(q, kv, k_rope, *, batch, seq, tq=512):
    t = q.shape[0]
    tq = _pick(seq, tq)
    return pl.pallas_call(
        functools.partial(_softmax_attn_kernel, tq=tq, mode="mla"),
        grid=(batch, N_HEADS),
        in_specs=[pl.BlockSpec((seq, 2 * LANES), lambda b, h: (b, h)),
                  pl.BlockSpec((seq, HEAD_DIM), lambda b, h: (b, h)),
                  pl.BlockSpec((seq, LANES), lambda b, h: (b, 0)),
                  pl.BlockSpec((seq, HEAD_DIM), lambda b, h: (b, N_HEADS + h))],
        out_specs=pl.BlockSpec((seq, HEAD_DIM), lambda b, h: (b, h)),
        out_shape=jax.ShapeDtypeStruct((t, N_HEADS * HEAD_DIM), BF16),
        compiler_params=_params("parallel", "parallel"),
        name="mla_attention",
    )(q, kv, k_rope, kv)


def _sb_attn_kernel(q_ref, k_ref, v_ref, o_ref, *, tq):
    seq = q_ref.shape[0]
    row = lax.broadcasted_iota(jnp.int32, (tq, tq), 0)
    col = lax.broadcasted_iota(jnp.int32, (tq, tq), 1)
    strict = col < row
    neg_after = jnp.where(row > col, -1.0, 0.0).astype(BF16)
    neg_after2 = jnp.concatenate([neg_after, neg_after], axis=0)

    def blk(i):
        return slice(i * tq, (i + 1) * tq)

    tiles = [(qi, kj) for qi in range(seq // tq) for kj in range(qi, -1, -1)]
    scores, mids = {}, {}
    acc = later = None
    for n in range(len(tiles) + 2):
        if n < len(tiles):
            qi, kj = tiles[n]
            scores[n] = lax.dot_general(q_ref[blk(qi), :], k_ref[blk(kj), :],
                                        (((1,), (1,)), ((), ())), preferred_element_type=F32)
        if 0 <= n - 1 < len(tiles):
            qi, kj = tiles[n - 1]
            z = scores.pop(n - 1)
            sp = jnp.maximum(z, 0.0) + jnp.log(1.0 + jnp.exp2(jnp.abs(z) * -LOG2_E))
            if kj == qi:
                sp = jnp.where(strict, sp, 0.0)
            hi = sp.astype(BF16)
            lo = (sp - hi.astype(F32)).astype(BF16)
            after = jnp.dot(jnp.concatenate([hi, lo], axis=1), neg_after2,
                            preferred_element_type=F32)
            mids[n - 1] = (z - sp, after, sp[:, 0:1])
        if 0 <= n - 2 < len(tiles):
            qi, kj = tiles[n - 2]
            log_beta, after, sp0 = mids.pop(n - 2)
            if kj == qi:
                acc = jnp.zeros((tq, HEAD_DIM), F32)
                later = jnp.zeros((tq, 1), F32)
            a = jnp.exp((log_beta + after) + later)
            if kj == qi:
                a = jnp.where(strict, a, 0.0)
            acc = acc + jnp.dot(a.astype(BF16), v_ref[blk(kj), :], preferred_element_type=F32)
            later = later + (after[:, 0:1] - sp0)
            if kj == 0:
                o_ref[blk(qi), :] = acc.astype(o_ref.dtype)


def _sb_attention(qkv, *, batch, seq, tq=256):
    t = qkv.shape[0]
    tq = _pick(seq, tq)
    blk = lambda off: pl.BlockSpec((seq, HEAD_DIM), lambda b, h: (b, off + h))
    return pl.pallas_call(
        functools.partial(_sb_attn_kernel, tq=tq),
        grid=(batch, N_HEADS),
        in_specs=[blk(0), blk(N_HEADS), blk(2 * N_HEADS)],
        out_specs=blk(0),
        out_shape=jax.ShapeDtypeStruct((t, N_HEADS * HEAD_DIM), BF16),
        compiler_params=_params("parallel", "parallel"),
        name="sb_attention",
    )(qkv, qkv, qkv)


def _sgu_kernel(u_ref, v_ref, g_ref, ws_ref, bs_ref, o_ref):
    tm, width = v_ref.shape
    row = lax.broadcasted_iota(jnp.int32, (SGU_CHUNK, SGU_CHUNK), 0)
    col = lax.broadcasted_iota(jnp.int32, (SGU_CHUNK, SGU_CHUNK), 1)
    causal = col <= row
    vn = _rms(v_ref[...].astype(F32), g_ref[...]).astype(BF16)
    bs = bs_ref[...]
    for g in range(width // LANES):
        cs = slice(g * LANES, (g + 1) * LANES)
        w = jnp.where(causal, ws_ref[g], 0.0).astype(BF16)
        for n in range(tm // SGU_CHUNK):
            rs = slice(n * SGU_CHUNK, (n + 1) * SGU_CHUNK)
            mixed = jnp.dot(w, vn[rs, cs], preferred_element_type=F32) + bs[:, g:g + 1]
            o_ref[rs, cs] = (u_ref[rs, cs].astype(F32) * mixed).astype(o_ref.dtype)


def _sgu_spatial(uv, v_gain, w_s, b_s, *, tm=512):
    t = uv.shape[0]
    width = uv.shape[1] // 2
    tm = _pick(t, tm)
    return pl.pallas_call(
        _sgu_kernel,
        grid=(t // tm,),
        in_specs=[pl.BlockSpec((tm, width), lambda i: (i, 0)),
                  pl.BlockSpec((tm, width), lambda i: (i, 1)),
                  pl.BlockSpec((1, width), lambda i: (0, 0)),
                  pl.BlockSpec(w_s.shape, lambda i: (0, 0, 0)),
                  pl.BlockSpec((SGU_CHUNK, SGU_GROUPS), lambda i: (0, 0))],
        out_specs=pl.BlockSpec((tm, width), lambda i: (i, 0)),
        out_shape=jax.ShapeDtypeStruct((t, width), BF16),
        compiler_params=_params("parallel"),
        name="sgu_spatial",
    )(uv, uv, v_gain.reshape(1, width), w_s, b_s.T)


def _tile_gain(gain, reps):
    return jnp.tile(gain.astype(F32), reps)


def _qkv_mixer_proj(h, norm_gain, w_qkv, q_gain, k_gain, scale, name):
    hd = N_HEADS * HEAD_DIM
    tn = 1024
    per = tn // HEAD_DIM
    gains = jnp.stack([_tile_gain(q_gain * scale, per), _tile_gain(k_gain, per),
                       jnp.ones((tn,), F32)]).reshape(3, 1, tn)
    tiles_per_seg = hd // tn
    epi = functools.partial(_headnorm_epilogue, n_norm_tiles=2 * tiles_per_seg)
    return _norm_proj(
        h, norm_gain, w_qkv.astype(BF16), epi, BF16, k=h.shape[1], tn=tn,
        extras=(gains,),
        extra_specs=(pl.BlockSpec((None, 1, tn), lambda i, j: (j // tiles_per_seg, 0, 0)),),
        name=name)


def _fox_mixer(h, norm_gain, w_in, b_f, q_gain, k_gain, w_out, *, batch, seq):
    hd = N_HEADS * HEAD_DIM
    d = h.shape[1]
    qkv = _qkv_mixer_proj(h, norm_gain, w_in[:, :3 * hd], q_gain, k_gain, HEAD_DIM ** -0.5,
                          "fox_qkv_proj")
    w_f = jnp.pad(w_in[:, 3 * hd:], ((0, 0), (0, LANES - N_HEADS))).astype(BF16)
    f_logit = _norm_proj(h, norm_gain, w_f, _plain_epilogue, F32, k=d, tn=LANES,
                         name="fox_fgate_proj")
    b_pad = jnp.pad(b_f.astype(F32), (0, LANES - N_HEADS)).reshape(1, LANES)
    cum = _fox_cum(f_logit, b_pad, batch=batch, seq=seq)
    o = _fox_attention(qkv, cum, batch=batch, seq=seq)
    return _proj_residual(o, w_out.astype(BF16), h, name="fox_out_proj")


def _mla_mixer(h, positions, norm_gain, w_in, q_a_gain, kv_a_gain, w_q_b, w_kv_b, q_gain,
               k_gain, w_out, *, batch, seq):
    t, d = h.shape
    n_in = MLA_Q_RANK + MLA_KV_RANK + MLA_ROPE
    n_in_pad = MLA_Q_RANK + MLA_KV_RANK + LANES
    w_in_p = jnp.pad(w_in, ((0, 0), (0, n_in_pad - n_in))).astype(BF16)
    y = _norm_proj(h, norm_gain, w_in_p, _plain_epilogue, F32, k=d, tn=n_in_pad,
                   name="mla_down_proj")

    half = MLA_ROPE // 2
    inv_freq = ROPE_THETA ** (-jnp.arange(0, MLA_ROPE, 2, dtype=F32) / MLA_ROPE)
    zeros = jnp.zeros((LANES - MLA_ROPE,), F32)
    inv_row = jnp.concatenate([inv_freq, inv_freq, zeros]).reshape(1, LANES)
    cmask = jnp.concatenate([jnp.ones((MLA_ROPE,), F32), zeros]).reshape(1, LANES)
    ssign = jnp.concatenate([-jnp.ones((half,), F32), jnp.ones((half,), F32), zeros]).reshape(1, LANES)
    tm_small = _pick(t, 1024)
    row_spec = pl.BlockSpec((tm_small, LANES), lambda i: (i, 0))
    const_spec = pl.BlockSpec((1, LANES), lambda i: (0, 0))
    cos_t, sin_t = pl.pallas_call(
        _rope_table_kernel,
        grid=(t // tm_small,),
        in_specs=[pl.BlockSpec((tm_small, 1), lambda i: (i, 0)), const_spec, const_spec, const_spec],
        out_specs=[row_spec, row_spec],
        out_shape=[jax.ShapeDtypeStruct((t, LANES), F32)] * 2,
        compiler_params=_params("parallel"),
        name="rope_tables",
    )(positions.reshape(t, 1), inv_row, cmask, ssign)

    scale = (MLA_NOPE + MLA_ROPE) ** -0.5
    wq = w_q_b.reshape(MLA_Q_RANK, N_HEADS, MLA_NOPE + MLA_ROPE)
    wq = jnp.pad(wq, ((0, 0), (0, 0), (0, 2 * LANES - MLA_NOPE - MLA_ROPE)))
    wq = wq.reshape(MLA_Q_RANK, N_HEADS * 2 * LANES).astype(BF16)
    tn = 1024
    q_head_gain = jnp.concatenate([q_gain.astype(F32), zeros])
    q_gains = _tile_gain(q_head_gain, tn // (2 * LANES)).reshape(1, tn)
    tm = _pick(t, 1024)
    tab_spec = pl.BlockSpec((tm, LANES), lambda i, j: (i, 0))
    q = _norm_proj(y, q_a_gain, wq, functools.partial(_mla_q_epilogue, scale=scale), BF16,
                   k=MLA_Q_RANK, x_col_block=0, tm=tm, tn=tn,
                   extras=(q_gains, cos_t, sin_t),
                   extra_specs=(pl.BlockSpec((1, tn), lambda i, j: (0, 0)), tab_spec, tab_spec),
                   name="mla_q_proj")

    wkv = w_kv_b.reshape(MLA_KV_RANK, N_HEADS, MLA_NOPE + HEAD_DIM)
    wkv = jnp.concatenate([wkv[:, :, :MLA_NOPE].reshape(MLA_KV_RANK, -1),
                           wkv[:, :, MLA_NOPE:].reshape(MLA_KV_RANK, -1)], axis=1).astype(BF16)
    k_gains = jnp.stack([_tile_gain(k_gain[:MLA_NOPE], tn // HEAD_DIM),
                         jnp.ones((tn,), F32)]).reshape(2, 1, tn)
    tiles_per_seg = N_HEADS * HEAD_DIM // tn
    kv = _norm_proj(y, kv_a_gain, wkv,
                    functools.partial(_headnorm_epilogue, n_norm_tiles=tiles_per_seg), BF16,
                    k=MLA_KV_RANK, x_col_block=1, tn=tn,
                    extras=(k_gains,),
                    extra_specs=(pl.BlockSpec((None, 1, tn), lambda i, j: (j // tiles_per_seg, 0, 0)),),
                    name="mla_kv_proj")

    kr_gain = jnp.concatenate([k_gain[MLA_NOPE:].astype(F32), zeros]).reshape(1, LANES)
    k_rope = pl.pallas_call(
        _krope_kernel,
        grid=(t // tm_small,),
        in_specs=[pl.BlockSpec((tm_small, LANES), lambda i: (i, (MLA_Q_RANK + MLA_KV_RANK) // LANES)),
                  const_spec, row_spec, row_spec],
        out_specs=row_spec,
        out_shape=jax.ShapeDtypeStruct((t, LANES), BF16),
        compiler_params=_params("parallel"),
        name="mla_k_rope",
    )(y, kr_gain, cos_t, sin_t)

    o = _mla_attention(q, kv, k_rope, batch=batch, seq=seq)
    return _proj_residual(o, w_out.astype(BF16), h, name="mla_out_proj")


def _sb_mixer(h, norm_gain, w_in, q_gain, k_gain, w_out, *, batch, seq):
    qkv = _qkv_mixer_proj(h, norm_gain, w_in, q_gain, k_gain, HEAD_DIM ** -0.5, "sb_qkv_proj")
    o = _sb_attention(qkv, batch=batch, seq=seq)
    return _proj_residual(o, w_out.astype(BF16), h, name="sb_out_proj")


def _sgu_mixer(h, norm_gain, w_in, v_gain, w_s, b_s, w_out):
    uv = _norm_proj(h, norm_gain, w_in.astype(BF16), _gelu_epilogue, BF16, k=h.shape[1],
                    name="sgu_in_proj")
    gated = _sgu_spatial(uv, v_gain, w_s, b_s)
    return _proj_residual(gated, w_out.astype(BF16), h, name="sgu_out_proj")


def kernel(x, positions, mix_norm, ffn_norm, fox_w_in, fox_b_f, fox_q_gain, fox_k_gain, fox_w_out, mla_w_in, mla_q_a_gain, mla_kv_a_gain, mla_w_q_b, mla_w_kv_b, mla_q_gain, mla_k_gain, mla_w_out, sb_w_in, sb_q_gain, sb_k_gain, sb_w_out, sgu_w_in, sgu_v_gain, sgu_w_s, sgu_b_s, sgu_w_out, ffn_w_up, ffn_conv_w, ffn_conv_b, ffn_w_down):
    batch, seq, d = x.shape
    depth = mix_norm.shape[0]
    n_mixers = 4
    h = x.reshape(batch * seq, d)
    for i in range(depth):
        m, j = i % n_mixers, i // n_mixers
        if m == 0:
            h = _fox_mixer(h, mix_norm[i], fox_w_in[j], fox_b_f[j], fox_q_gain[j], fox_k_gain[j],
                           fox_w_out[j], batch=batch, seq=seq)
        elif m == 1:
            h = _mla_mixer(h, positions, mix_norm[i], mla_w_in[j], mla_q_a_gain[j],
                           mla_kv_a_gain[j], mla_w_q_b[j], mla_w_kv_b[j], mla_q_gain[j],
                           mla_k_gain[j], mla_w_out[j], batch=batch, seq=seq)
        elif m == 2:
            h = _sb_mixer(h, mix_norm[i], sb_w_in[j], sb_q_gain[j], sb_k_gain[j], sb_w_out[j],
                          batch=batch, seq=seq)
        else:
            h = _sgu_mixer(h, mix_norm[i], sgu_w_in[j], sgu_v_gain[j], sgu_w_s[j], sgu_b_s[j],
                           sgu_w_out[j])
        h = _ffn(h, ffn_norm[i], ffn_w_up[i].astype(BF16), ffn_conv_w[i], ffn_conv_b[i],
                 ffn_w_down[i].astype(BF16), seq=seq)
    return h.reshape(batch, seq, d)
```

```python
import functools

import jax
import jax.numpy as jnp
from jax import lax
from jax.experimental import pallas as pl
from jax.experimental.pallas import tpu as pltpu

F32 = jnp.float32
BF16 = jnp.bfloat16

EPS = 1e-6
N_HEADS = 16
HEAD_DIM = 128
LANES = 128
SUBLANES = 8
MLA_Q_RANK = 512
MLA_KV_RANK = 512
MLA_NOPE = 128
MLA_ROPE = 64
ROPE_THETA = 10000.0
SGU_CHUNK = 128
SGU_GROUPS = 16
CONV_WIDTH = 3
LOG2_E = 1.4426950408889634
VMEM_LIMIT_BYTES = 56 * 1024 * 1024


def _params(*sem, flags=None):
    return pltpu.CompilerParams(dimension_semantics=sem, vmem_limit_bytes=VMEM_LIMIT_BYTES,
                                flags=flags)


def _pick(total, pref):
    t = min(pref, total)
    while total % t:
        t //= 2
    return t


def _rms(x, gain):
    ms = jnp.mean(x * x, axis=-1, keepdims=True)
    return (x * lax.rsqrt(ms + EPS)) * gain


def _norm_proj_kernel(x_ref, g_ref, w_ref, *rest, epilogue, n_extra):
    extra = rest[:n_extra]
    o_ref = rest[n_extra]
    xn_ref = rest[n_extra + 1]
    j = pl.program_id(1)

    @pl.when(j == 0)
    def _():
        xn_ref[...] = _rms(x_ref[...], g_ref[...]).astype(BF16)

    acc = jnp.dot(xn_ref[...], w_ref[...], preferred_element_type=F32)
    epilogue(acc, j, o_ref, *extra)


def _norm_proj(x, gain, w, epilogue, out_dtype, *, k, x_col_block=0, tm=1024, tn=1024,
               extras=(), extra_specs=(), name):
    t = x.shape[0]
    n = w.shape[1]
    tm = _pick(t, tm)
    tn = _pick(n, tn)
    kern = functools.partial(_norm_proj_kernel, epilogue=epilogue, n_extra=len(extras))
    return pl.pallas_call(
        kern,
        grid=(t // tm, n // tn),
        in_specs=[
            pl.BlockSpec((tm, k), lambda i, j: (i, x_col_block)),
            pl.BlockSpec((1, k), lambda i, j: (0, 0)),
            pl.BlockSpec((k, tn), lambda i, j: (0, j)),
            *extra_specs,
        ],
        out_specs=pl.BlockSpec((tm, tn), lambda i, j: (i, j)),
        out_shape=jax.ShapeDtypeStruct((t, n), out_dtype),
        scratch_shapes=[pltpu.VMEM((tm, k), BF16)],
        compiler_params=_params("parallel", "arbitrary"),
        name=name,
    )(x, gain.reshape(1, k), w, *extras)


def _plain_epilogue(acc, j, o_ref):
    o_ref[...] = acc.astype(o_ref.dtype)


def _gelu_epilogue(acc, j, o_ref):
    c = 0.7978845608028654
    cdf = 0.5 * (1.0 + jnp.tanh(c * (acc + 0.044715 * (acc * acc * acc))))
    o_ref[...] = (acc * cdf).astype(o_ref.dtype)


def _headnorm_epilogue(acc, j, o_ref, hg_ref, *, n_norm_tiles):
    tn = acc.shape[1]

    @pl.when(j < n_norm_tiles)
    def _():
        g = hg_ref[...]
        for hh in range(tn // HEAD_DIM):
            sl = slice(hh * HEAD_DIM, (hh + 1) * HEAD_DIM)
            o_ref[:, sl] = _rms(acc[:, sl], g[:, sl]).astype(o_ref.dtype)

    @pl.when(j >= n_norm_tiles)
    def _():
        o_ref[...] = acc.astype(o_ref.dtype)


def _rope(x, cos_t, sin_t):
    lane = lax.broadcasted_iota(jnp.int32, x.shape, 1)
    half = MLA_ROPE // 2
    partner = jnp.where(lane < half, pltpu.roll(x, LANES - half, axis=1),
                        pltpu.roll(x, half, axis=1))
    return x * cos_t + partner * sin_t


def _rms_rope(x, gain, cos_t, sin_t):
    ms = jnp.sum(x * x, axis=-1, keepdims=True) * (1.0 / MLA_ROPE)
    return _rope((x * lax.rsqrt(ms + EPS)) * gain, cos_t, sin_t)


def _mla_q_epilogue(acc, j, o_ref, hg_ref, cos_ref, sin_ref, *, scale):
    tn = acc.shape[1]
    g = hg_ref[...]
    cos_t = cos_ref[...]
    sin_t = sin_ref[...]
    for hh in range(tn // (2 * LANES)):
        a = slice(hh * 2 * LANES, hh * 2 * LANES + LANES)
        b = slice(hh * 2 * LANES + LANES, (hh + 1) * 2 * LANES)
        o_ref[:, a] = (_rms(acc[:, a], g[:, a]) * scale).astype(o_ref.dtype)
        o_ref[:, b] = (_rms_rope(acc[:, b], g[:, b], cos_t, sin_t) * scale).astype(o_ref.dtype)


def _krope_kernel(y_ref, g_ref, cos_ref, sin_ref, o_ref):
    o_ref[...] = _rms_rope(y_ref[...], g_ref[...], cos_ref[...], sin_ref[...]).astype(o_ref.dtype)


def _rope_table_kernel(pos_ref, inv_ref, cmask_ref, ssign_ref, cos_ref, sin_ref):
    ang = pos_ref[...].astype(F32) * inv_ref[...]
    cos_ref[...] = jnp.cos(ang) * cmask_ref[...]
    sin_ref[...] = jnp.sin(ang) * ssign_ref[...]


def _proj_res_kernel(x_ref, w_ref, r_ref, o_ref):
    o_ref[...] = r_ref[...] + jnp.dot(x_ref[...], w_ref[...], preferred_element_type=F32)


def _proj_residual(x, w, res, *, tm=512, tn=2048, name):
    t, k = x.shape
    n = w.shape[1]
    tm = _pick(t, tm)
    tn = _pick(n, tn)
    return pl.pallas_call(
        _proj_res_kernel,
        grid=(t // tm, n // tn),
        in_specs=[
            pl.BlockSpec((tm, k), lambda i, j: (i, 0)),
            pl.BlockSpec((k, tn), lambda i, j: (0, j)),
            pl.BlockSpec((tm, tn), lambda i, j: (i, j)),
        ],
        out_specs=pl.BlockSpec((tm, tn), lambda i, j: (i, j)),
        out_shape=jax.ShapeDtypeStruct((t, n), F32),
        compiler_params=_params("parallel", "parallel"),
        name=name,
    )(x, w, res)


def _ffn_jobs(s, n_jobs, nf):
    job_a = jnp.minimum(s, n_jobs - 1)
    job_b = jnp.maximum(s - 1, 0)
    return job_a // nf, job_a % nf, job_b // nf, job_b % nf


def _ffn_kernel(h_ref, g_ref, wg_ref, wv_ref, cwg_ref, cwv_ref, cbg_ref, cbv_ref, wd_ref,
                o_ref, xn_ref, u0_ref, u1_ref, act_ref, carry_ref, *, nf, n_jobs, tiles_per_seq):
    s = pl.program_id(0)
    tm = o_ref.shape[0]
    tf = wg_ref.shape[1]
    _, ca, tb, cb = _ffn_jobs(s, n_jobs, nf)
    pad = SUBLANES
    sub = min(tf, 2 * LANES)

    @pl.when(s == 0)
    def _():
        u1_ref[...] = jnp.zeros(u1_ref.shape, F32)

    @pl.when(ca == 0)
    def _():
        xn_ref[...] = _rms(h_ref[...], g_ref[...]).astype(BF16)

    @pl.when(cb == 0)
    def _():
        o_ref[...] = h_ref[...]

    @pl.when(tb % tiles_per_seq == 0)
    def _():
        carry_ref[cb] = jnp.zeros(carry_ref.shape[1:], F32)

    def conv(u_ref, cols, taps, bias):
        y = bias + taps[0:1] * u_ref[pad - 2:pad - 2 + tm, cols]
        y = y + taps[1:2] * u_ref[pad - 1:pad - 1 + tm, cols]
        y = y + taps[2:3] * u_ref[pad:pad + tm, cols]
        return y

    def step(u_write, u_read):
        xn = xn_ref[...]
        u_write[pad:pad + tm, 0:tf] = jnp.dot(xn, wg_ref[...], preferred_element_type=F32)
        u_write[pad:pad + tm, tf:2 * tf] = jnp.dot(xn, wv_ref[...], preferred_element_type=F32)
        u_read[0:pad, :] = carry_ref[cb]
        for c in range(tf // sub):
            gs = slice(c * sub, (c + 1) * sub)
            vs = slice(tf + c * sub, tf + (c + 1) * sub)
            gate = conv(u_read, gs, cwg_ref[:, gs], cbg_ref[:, gs])
            val = conv(u_read, vs, cwv_ref[:, gs], cbv_ref[:, gs])
            act_ref[:, gs] = ((gate * (1.0 / (1.0 + jnp.exp(-gate)))) * val).astype(BF16)
            o_ref[...] += jnp.dot(act_ref[:, gs], wd_ref[gs, :], preferred_element_type=F32)
        carry_ref[cb] = u_read[tm:tm + pad, :]

    @pl.when(s % 2 == 0)
    def _():
        step(u0_ref, u1_ref)

    @pl.when(s % 2 == 1)
    def _():
        step(u1_ref, u0_ref)


def _ffn(h, gain, w_up, conv_w, conv_b, w_down, *, seq, tm=512, tf=512):
    t, d = h.shape
    d_ff = w_down.shape[0]
    tm = _pick(seq, tm)
    tf = _pick(d_ff, tf)
    nf = d_ff // tf
    assert nf >= 2, "the output tile is initialised from the h block one step into a tile"
    n_jobs = (t // tm) * nf
    kern = functools.partial(_ffn_kernel, nf=nf, n_jobs=n_jobs, tiles_per_seq=seq // tm)
    jobs = functools.partial(_ffn_jobs, n_jobs=n_jobs, nf=nf)
    u_shape = (SUBLANES + tm, 2 * tf)
    return pl.pallas_call(
        kern,
        grid=(n_jobs + 1,),
        in_specs=[
            pl.BlockSpec((tm, d), lambda s: (jobs(s)[0], 0)),
            pl.BlockSpec((1, d), lambda s: (0, 0)),
            pl.BlockSpec((d, tf), lambda s: (0, jobs(s)[1])),
            pl.BlockSpec((d, tf), lambda s: (0, nf + jobs(s)[1])),
            pl.BlockSpec((CONV_WIDTH, tf), lambda s: (0, jobs(s)[3])),
            pl.BlockSpec((CONV_WIDTH, tf), lambda s: (0, nf + jobs(s)[3])),
            pl.BlockSpec((1, tf), lambda s: (0, jobs(s)[3])),
            pl.BlockSpec((1, tf), lambda s: (0, nf + jobs(s)[3])),
            pl.BlockSpec((tf, d), lambda s: (jobs(s)[3], 0)),
        ],
        out_specs=pl.BlockSpec((tm, d), lambda s: (jobs(s)[2], 0)),
        out_shape=jax.ShapeDtypeStruct((t, d), F32),
        scratch_shapes=[pltpu.VMEM((tm, d), BF16), pltpu.VMEM(u_shape, F32),
                        pltpu.VMEM(u_shape, F32), pltpu.VMEM((tm, tf), BF16),
                        pltpu.VMEM((nf, SUBLANES, 2 * tf), F32)],
        compiler_params=_params("arbitrary"),
        name="conv_ffn",
    )(h, gain.reshape(1, d), w_up, w_up, conv_w, conv_w, conv_b.reshape(1, -1),
      conv_b.reshape(1, -1), w_down)


def _fox_cum_kernel(f_ref, b_ref, o_ref, *, chunk):
    s = f_ref.shape[0]
    r = lax.broadcasted_iota(jnp.int32, (chunk, chunk), 0)
    c = lax.broadcasted_iota(jnp.int32, (chunk, chunk), 1)
    tri = (r <= c).astype(BF16)
    carry = jnp.zeros((LANES, 1), F32)
    for ci in range(s // chunk):
        x = f_ref[ci * chunk:(ci + 1) * chunk, :] + b_ref[...]
        lf = jnp.minimum(x, 0.0) - jnp.log1p(jnp.exp(-jnp.abs(x)))
        lft = lf.T
        hi = lft.astype(BF16)
        r1 = lft - hi.astype(F32)
        mid = r1.astype(BF16)
        lo = (r1 - mid.astype(F32)).astype(BF16)
        local = (jnp.dot(hi, tri, preferred_element_type=F32)
                 + jnp.dot(mid, tri, preferred_element_type=F32)
                 + jnp.dot(lo, tri, preferred_element_type=F32)) + carry
        o_ref[:, ci * chunk:(ci + 1) * chunk] = local[:N_HEADS]
        carry = local[:, chunk - 1:chunk]


def _fox_cum(f_logit, b_f, *, batch, seq):
    chunk = _pick(seq, 512)
    return pl.pallas_call(
        functools.partial(_fox_cum_kernel, chunk=chunk),
        grid=(batch,),
        in_specs=[pl.BlockSpec((seq, LANES), lambda b: (b, 0)),
                  pl.BlockSpec((1, LANES), lambda b: (0, 0))],
        out_specs=pl.BlockSpec((None, N_HEADS, seq), lambda b: (b, 0, 0)),
        out_shape=jax.ShapeDtypeStruct((batch, N_HEADS, seq), F32),
        compiler_params=_params("parallel"),
        name="fox_cum",
    )(f_logit, b_f)


def _softmax_attn_kernel(*refs, tq, mode):
    if mode == "fox":
        q_ref, k_ref, v_ref, c_ref, o_ref = refs
    else:
        q_ref, k_ref, kr_ref, v_ref, o_ref = refs
    h = pl.program_id(1)
    seq = q_ref.shape[0]
    row = lax.broadcasted_iota(jnp.int32, (tq, tq), 0)
    col = lax.broadcasted_iota(jnp.int32, (tq, tq), 1)
    allowed = col <= row

    def blk(i):
        return slice(i * tq, (i + 1) * tq)

    tiles = [(qi, kj) for qi in range(seq // tq) for kj in range(qi + 1)]
    scores = {}
    m = l = acc = None
    for n in range(len(tiles) + 1):
        if n < len(tiles):
            qi, kj = tiles[n]
            k = k_ref[blk(kj), :]
            if mode == "mla":
                k = jnp.concatenate([k, kr_ref[blk(kj), :]], axis=1)
            scores[n] = lax.dot_general(q_ref[blk(qi), :], k, (((1,), (1,)), ((), ())),
                                        preferred_element_type=F32)
        if n >= 1:
            qi, kj = tiles[n - 1]
            s = scores.pop(n - 1)
            if kj == 0:
                m = jnp.full((tq, 1), -jnp.inf, F32)
                l = jnp.zeros((tq, 1), F32)
                acc = jnp.zeros((tq, HEAD_DIM), F32)
            if mode == "fox":
                s = s - c_ref[pl.ds(h, 1), blk(kj)]
            if kj == qi:
                s = jnp.where(allowed, s, -jnp.inf)
            m_new = jnp.maximum(m, jnp.max(s, axis=1, keepdims=True))
            alpha = jnp.exp(m - m_new)
            p = jnp.exp(s - m_new)
            l = alpha * l + jnp.sum(p, axis=1, keepdims=True)
            acc = alpha * acc + jnp.dot(p.astype(BF16), v_ref[blk(kj), :],
                                        preferred_element_type=F32)
            m = m_new
            if kj == qi:
                o_ref[blk(qi), :] = (acc / l).astype(o_ref.dtype)


def _fox_attention(qkv, cum, *, batch, seq, tq=512):
    t = qkv.shape[0]
    tq = _pick(seq, tq)
    blk = lambda off: pl.BlockSpec((seq, HEAD_DIM), lambda b, h: (b, off + h))
    return pl.pallas_call(
        functools.partial(_softmax_attn_kernel, tq=tq, mode="fox"),
        grid=(batch, N_HEADS),
        in_specs=[blk(0), blk(N_HEADS), blk(2 * N_HEADS),
                  pl.BlockSpec((None, N_HEADS, seq), lambda b, h: (b, 0, 0))],
        out_specs=blk(0),
        out_shape=jax.ShapeDtypeStruct((t, N_HEADS * HEAD_DIM), BF16),
        compiler_params=_params("parallel", "parallel"),
        name="fox_attention",
    )(qkv, qkv, qkv, cum)


def _mla_attention(q, kv, k_rope, *, batch, seq, tq=512):
    t = q.shape[0]
    tq = _pick(seq, tq)
    return pl.pallas_call(
        functools.partial(_softmax_attn_kernel, tq=tq, mode="mla"),
        grid=(batch, N_HEADS),
        in_specs=[pl.BlockSpec((seq, 2 * LANES), lambda b, h: (b, h)),
                  pl.BlockSpec((seq, HEAD_DIM), lambda b, h: (b, h)),
                  pl.BlockSpec((seq, LANES), lambda b, h: (b, 0)),
                  pl.BlockSpec((seq, HEAD_DIM), lambda b, h: (b, N_HEADS + h))],
        out_specs=pl.BlockSpec((seq, HEAD_DIM), lambda b, h: (b, h)),
        out_shape=jax.ShapeDtypeStruct((t, N_HEADS * HEAD_DIM), BF16),
        compiler_params=_params("parallel", "parallel"),
        name="mla_attention",
    )(q, kv, k_rope, kv)


def _sb_attn_kernel(q_ref, k_ref, v_ref, o_ref, *, tq):
    seq = q_ref.shape[0]
    row = lax.broadcasted_iota(jnp.int32, (tq, tq), 0)
    col = lax.broadcasted_iota(jnp.int32, (tq, tq), 1)
    strict = col < row
    neg_after = jnp.where(row > col, -1.0, 0.0).astype(BF16)
    neg_after2 = jnp.concatenate([neg_after, neg_after], axis=0)

    def blk(i):
        return slice(i * tq, (i + 1) * tq)

    tiles = [(qi, kj) for qi in range(seq // tq) for kj in range(qi, -1, -1)]
    scores, mids = {}, {}
    acc = later = None
    for n in range(len(tiles) + 2):
        if n < len(tiles):
            qi, kj = tiles[n]
            scores[n] = lax.dot_general(q_ref[blk(qi), :], k_ref[blk(kj), :],
                                        (((1,), (1,)), ((), ())), preferred_element_type=F32)
        if 0 <= n - 1 < len(tiles):
            qi, kj = tiles[n - 1]
            z = scores.pop(n - 1)
            sp = jnp.maximum(z, 0.0) + jnp.log(1.0 + jnp.exp2(jnp.abs(z) * -LOG2_E))
            if kj == qi:
                sp = jnp.where(strict, sp, 0.0)
            hi = sp.astype(BF16)
            lo = (sp - hi.astype(F32)).astype(BF16)
            after = jnp.dot(jnp.concatenate([hi, lo], axis=1), neg_after2,
                            preferred_element_type=F32)
            mids[n - 1] = (z - sp, after, sp[:, 0:1])
        if 0 <= n - 2 < len(tiles):
            qi, kj = tiles[n - 2]
            log_beta, after, sp0 = mids.pop(n - 2)
            if kj == qi:
                acc = jnp.zeros((tq, HEAD_DIM), F32)
                later = jnp.zeros((tq, 1), F32)
            a = jnp.exp((log_beta + after) + later)
            if kj == qi:
                a = jnp.where(strict, a, 0.0)
            acc = acc + jnp.dot(a.astype(BF16), v_ref[blk(kj), :], preferred_element_type=F32)
            later = later + (after[:, 0:1] - sp0)
            if kj == 0:
                o_ref[blk(qi), :] = acc.astype(o_ref.dtype)


def _sb_attention(qkv, *, batch, seq, tq=256):
    t = qkv.shape[0]
    tq = _pick(seq, tq)
    blk = lambda off: pl.BlockSpec((seq, HEAD_DIM), lambda b, h: (b, off + h))
    return pl.pallas_call(
        functools.partial(_sb_attn_kernel, tq=tq),
        grid=(batch, N_HEADS),
        in_specs=[blk(0), blk(N_HEADS), blk(2 * N_HEADS)],
        out_specs=blk(0),
        out_shape=jax.ShapeDtypeStruct((t, N_HEADS * HEAD_DIM), BF16),
        compiler_params=_params("parallel", "parallel"),
        name="sb_attention",
    )(qkv, qkv, qkv)


def _sgu_kernel(u_ref, v_ref, g_ref, ws_ref, bs_ref, o_ref):
    tm, width = v_ref.shape
    row = lax.broadcasted_iota(jnp.int32, (SGU_CHUNK, SGU_CHUNK), 0)
    col = lax.broadcasted_iota(jnp.int32, (SGU_CHUNK, SGU_CHUNK), 1)
    causal = col <= row
    vn = _rms(v_ref[...].astype(F32), g_ref[...]).astype(BF16)
    bs = bs_ref[...]
    for g in range(width // LANES):
        cs = slice(g * LANES, (g + 1) * LANES)
        w = jnp.where(causal, ws_ref[g], 0.0).astype(BF16)
        for n in range(tm // SGU_CHUNK):
            rs = slice(n * SGU_CHUNK, (n + 1) * SGU_CHUNK)
            mixed = jnp.dot(w, vn[rs, cs], preferred_element_type=F32) + bs[:, g:g + 1]
            o_ref[rs, cs] = (u_ref[rs, cs].astype(F32) * mixed).astype(o_ref.dtype)


def _sgu_spatial(uv, v_gain, w_s, b_s, *, tm=512):
    t = uv.shape[0]
    width = uv.shape[1] // 2
    tm = _pick(t, tm)
    return pl.pallas_call(
        _sgu_kernel,
        grid=(t // tm,),
        in_specs=[pl.BlockSpec((tm, width), lambda i: (i, 0)),
                  pl.BlockSpec((tm, width), lambda i: (i, 1)),
                  pl.BlockSpec((1, width), lambda i: (0, 0)),
                  pl.BlockSpec(w_s.shape, lambda i: (0, 0, 0)),
                  pl.BlockSpec((SGU_CHUNK, SGU_GROUPS), lambda i: (0, 0))],
        out_specs=pl.BlockSpec((tm, width), lambda i: (i, 0)),
        out_shape=jax.ShapeDtypeStruct((t, width), BF16),
        compiler_params=_params("parallel"),
        name="sgu_spatial",
    )(uv, uv, v_gain.reshape(1, width), w_s, b_s.T)


def _tile_gain(gain, reps):
    return jnp.tile(gain.astype(F32), reps)


def _qkv_mixer_proj(h, norm_gain, w_qkv, q_gain, k_gain, scale, name):
    hd = N_HEADS * HEAD_DIM
    tn = 1024
    per = tn // HEAD_DIM
    gains = jnp.stack([_tile_gain(q_gain * scale, per), _tile_gain(k_gain, per),
                       jnp.ones((tn,), F32)]).reshape(3, 1, tn)
    tiles_per_seg = hd // tn
    epi = functools.partial(_headnorm_epilogue, n_norm_tiles=2 * tiles_per_seg)
    return _norm_proj(
        h, norm_gain, w_qkv.astype(BF16), epi, BF16, k=h.shape[1], tn=tn,
        extras=(gains,),
        extra_specs=(pl.BlockSpec((None, 1, tn), lambda i, j: (j // tiles_per_seg, 0, 0)),),
        name=name)


def _fox_mixer(h, norm_gain, w_in, b_f, q_gain, k_gain, w_out, *, batch, seq):
    hd = N_HEADS * HEAD_DIM
    d = h.shape[1]
    qkv = _qkv_mixer_proj(h, norm_gain, w_in[:, :3 * hd], q_gain, k_gain, HEAD_DIM ** -0.5,
                          "fox_qkv_proj")
    w_f = jnp.pad(w_in[:, 3 * hd:], ((0, 0), (0, LANES - N_HEADS))).astype(BF16)
    f_logit = _norm_proj(h, norm_gain, w_f, _plain_epilogue, F32, k=d, tn=LANES,
                         name="fox_fgate_proj")
    b_pad = jnp.pad(b_f.astype(F32), (0, LANES - N_HEADS)).reshape(1, LANES)
    cum = _fox_cum(f_logit, b_pad, batch=batch, seq=seq)
    o = _fox_attention(qkv, cum, batch=batch, seq=seq)
    return _proj_residual(o, w_out.astype(BF16), h, name="fox_out_proj")


def _mla_mixer(h, positions, norm_gain, w_in, q_a_gain, kv_a_gain, w_q_b, w_kv_b, q_gain,
               k_gain, w_out, *, batch, seq):
    t, d = h.shape
    n_in = MLA_Q_RANK + MLA_KV_RANK + MLA_ROPE
    n_in_pad = MLA_Q_RANK + MLA_KV_RANK + LANES
    w_in_p = jnp.pad(w_in, ((0, 0), (0, n_in_pad - n_in))).astype(BF16)
    y = _norm_proj(h, norm_gain, w_in_p, _plain_epilogue, F32, k=d, tn=n_in_pad,
                   name="mla_down_proj")

    half = MLA_ROPE // 2
    inv_freq = ROPE_THETA ** (-jnp.arange(0, MLA_ROPE, 2, dtype=F32) / MLA_ROPE)
    zeros = jnp.zeros((LANES - MLA_ROPE,), F32)
    inv_row = jnp.concatenate([inv_freq, inv_freq, zeros]).reshape(1, LANES)
    cmask = jnp.concatenate([jnp.ones((MLA_ROPE,), F32), zeros]).reshape(1, LANES)
    ssign = jnp.concatenate([-jnp.ones((half,), F32), jnp.ones((half,), F32), zeros]).reshape(1, LANES)
    tm_small = _pick(t, 1024)
    row_spec = pl.BlockSpec((tm_small, LANES), lambda i: (i, 0))
    const_spec = pl.BlockSpec((1, LANES), lambda i: (0, 0))
    cos_t, sin_t = pl.pallas_call(
        _rope_table_kernel,
        grid=(t // tm_small,),
        in_specs=[pl.BlockSpec((tm_small, 1), lambda i: (i, 0)), const_spec, const_spec, const_spec],
        out_specs=[row_spec, row_spec],
        out_shape=[jax.ShapeDtypeStruct((t, LANES), F32)] * 2,
        compiler_params=_params("parallel"),
        name="rope_tables",
    )(positions.reshape(t, 1), inv_row, cmask, ssign)

    scale = (MLA_NOPE + MLA_ROPE) ** -0.5
    wq = w_q_b.reshape(MLA_Q_RANK, N_HEADS, MLA_NOPE + MLA_ROPE)
    wq = jnp.pad(wq, ((0, 0), (0, 0), (0, 2 * LANES - MLA_NOPE - MLA_ROPE)))
    wq = wq.reshape(MLA_Q_RANK, N_HEADS * 2 * LANES).astype(BF16)
    tn = 1024
    q_head_gain = jnp.concatenate([q_gain.astype(F32), zeros])
    q_gains = _tile_gain(q_head_gain, tn // (2 * LANES)).reshape(1, tn)
    tm = _pick(t, 1024)
    tab_spec = pl.BlockSpec((tm, LANES), lambda i, j: (i, 0))
    q = _norm_proj(y, q_a_gain, wq, functools.partial(_mla_q_epilogue, scale=scale), BF16,
                   k=MLA_Q_RANK, x_col_block=0, tm=tm, tn=tn,
                   extras=(q_gains, cos_t, sin_t),
                   extra_specs=(pl.BlockSpec((1, tn), lambda i, j: (0, 0)), tab_spec, tab_spec),
                   name="mla_q_proj")

    wkv = w_kv_b.reshape(MLA_KV_RANK, N_HEADS, MLA_NOPE + HEAD_DIM)
    wkv = jnp.concatenate([wkv[:, :, :MLA_NOPE].reshape(MLA_KV_RANK, -1),
                           wkv[:, :, MLA_NOPE:].reshape(MLA_KV_RANK, -1)], axis=1).astype(BF16)
    k_gains = jnp.stack([_tile_gain(k_gain[:MLA_NOPE], tn // HEAD_DIM),
                         jnp.ones((tn,), F32)]).reshape(2, 1, tn)
    tiles_per_seg = N_HEADS * HEAD_DIM // tn
    kv = _norm_proj(y, kv_a_gain, wkv,
                    functools.partial(_headnorm_epilogue, n_norm_tiles=tiles_per_seg), BF16,
                    k=MLA_KV_RANK, x_col_block=1, tn=tn,
                    extras=(k_gains,),
                    extra_specs=(pl.BlockSpec((None, 1, tn), lambda i, j: (j // tiles_per_seg, 0, 0)),),
                    name="mla_kv_proj")

    kr_gain = jnp.concatenate([k_gain[MLA_NOPE:].astype(F32), zeros]).reshape(1, LANES)
    k_rope = pl.pallas_call(
        _krope_kernel,
        grid=(t // tm_small,),
        in_specs=[pl.BlockSpec((tm_small, LANES), lambda i: (i, (MLA_Q_RANK + MLA_KV_RANK) // LANES)),
                  const_spec, row_spec, row_spec],
        out_specs=row_spec,
        out_shape=jax.ShapeDtypeStruct((t, LANES), BF16),
        compiler_params=_params("parallel"),
        name="mla_k_rope",
    )(y, kr_gain, cos_t, sin_t)

    o = _mla_attention(q, kv, k_rope, batch=batch, seq=seq)
    return _proj_residual(o, w_out.astype(BF16), h, name="mla_out_proj")


def _sb_mixer(h, norm_gain, w_in, q_gain, k_gain, w_out, *, batch, seq):
    qkv = _qkv_mixer_proj(h, norm_gain, w_in, q_gain, k_gain, HEAD_DIM ** -0.5, "sb_qkv_proj")
    o = _sb_attention(qkv, batch=batch, seq=seq)
    return _proj_residual(o, w_out.astype(BF16), h, name="sb_out_proj")


def _sgu_mixer(h, norm_gain, w_in, v_gain, w_s, b_s, w_out):
    uv = _norm_proj(h, norm_gain, w_in.astype(BF16), _gelu_epilogue, BF16, k=h.shape[1],
                    name="sgu_in_proj")
    gated = _sgu_spatial(uv, v_gain, w_s, b_s)
    return _proj_residual(gated, w_out.astype(BF16), h, name="sgu_out_proj")


def kernel(x, positions, mix_norm, ffn_norm, fox_w_in, fox_b_f, fox_q_gain, fox_k_gain, fox_w_out, mla_w_in, mla_q_a_gain, mla_kv_a_gain, mla_w_q_b, mla_w_kv_b, mla_q_gain, mla_k_gain, mla_w_out, sb_w_in, sb_q_gain, sb_k_gain, sb_w_out, sgu_w_in, sgu_v_gain, sgu_w_s, sgu_b_s, sgu_w_out, ffn_w_up, ffn_conv_w, ffn_conv_b, ffn_w_down):
    batch, seq, d = x.shape
    depth = mix_norm.shape[0]
    n_mixers = 4
    h = x.reshape(batch * seq, d)
    for i in range(depth):
        m, j = i % n_mixers, i // n_mixers
        if m == 0:
            h = _fox_mixer(h, mix_norm[i], fox_w_in[j], fox_b_f[j], fox_q_gain[j], fox_k_gain[j],
                           fox_w_out[j], batch=batch, seq=seq)
        elif m == 1:
            h = _mla_mixer(h, positions, mix_norm[i], mla_w_in[j], mla_q_a_gain[j],
                           mla_kv_a_gain[j], mla_w_q_b[j], mla_w_kv_b[j], mla_q_gain[j],
                           mla_k_gain[j], mla_w_out[j], batch=batch, seq=seq)
        elif m == 2:
            h = _sb_mixer(h, mix_norm[i], sb_w_in[j], sb_q_gain[j], sb_k_gain[j], sb_w_out[j],
                          batch=batch, seq=seq)
        else:
            h = _sgu_mixer(h, mix_norm[i], sgu_w_in[j], sgu_v_gain[j], sgu_w_s[j], sgu_b_s[j],
                           sgu_w_out[j])
        h = _ffn(h, ffn_norm[i], ffn_w_up[i].astype(BF16), ffn_conv_w[i], ffn_conv_b[i],
                 ffn_w_down[i].astype(BF16), seq=seq)
    return h.reshape(batch, seq, d)
```

```python
import functools

import jax
import jax.numpy as jnp
from jax import lax
from jax.experimental import pallas as pl
from jax.experimental.pallas import tpu as pltpu

F32 = jnp.float32
BF16 = jnp.bfloat16

EPS = 1e-6
N_HEADS = 16
HEAD_DIM = 128
LANES = 128
SUBLANES = 8
MLA_Q_RANK = 512
MLA_KV_RANK = 512
MLA_NOPE = 128
MLA_ROPE = 64
ROPE_THETA = 10000.0
SGU_CHUNK = 128
SGU_GROUPS = 16
CONV_WIDTH = 3
LOG2_E = 1.4426950408889634
VMEM_LIMIT_BYTES = 56 * 1024 * 1024


def _params(*sem, flags=None):
    return pltpu.CompilerParams(dimension_semantics=sem, vmem_limit_bytes=VMEM_LIMIT_BYTES,
                                flags=flags)


def _pick(total, pref):
    t = min(pref, total)
    while total % t:
        t //= 2
    return t


def _rms(x, gain):
    ms = jnp.mean(x * x, axis=-1, keepdims=True)
    return (x * lax.rsqrt(ms + EPS)) * gain


def _norm_proj_kernel(x_ref, g_ref, w_ref, *rest, epilogue, n_extra):
    extra = rest[:n_extra]
    o_ref = rest[n_extra]
    xn_ref = rest[n_extra + 1]
    j = pl.program_id(1)

    @pl.when(j == 0)
    def _():
        xn_ref[...] = _rms(x_ref[...], g_ref[...]).astype(BF16)

    acc = jnp.dot(xn_ref[...], w_ref[...], preferred_element_type=F32)
    epilogue(acc, j, o_ref, *extra)


def _norm_proj(x, gain, w, epilogue, out_dtype, *, k, x_col_block=0, tm=1024, tn=1024,
               extras=(), extra_specs=(), name):
    t = x.shape[0]
    n = w.shape[1]
    tm = _pick(t, tm)
    tn = _pick(n, tn)
    kern = functools.partial(_norm_proj_kernel, epilogue=epilogue, n_extra=len(extras))
    return pl.pallas_call(
        kern,
        grid=(t // tm, n // tn),
        in_specs=[
            pl.BlockSpec((tm, k), lambda i, j: (i, x_col_block)),
            pl.BlockSpec((1, k), lambda i, j: (0, 0)),
            pl.BlockSpec((k, tn), lambda i, j: (0, j)),
            *extra_specs,
        ],
        out_specs=pl.BlockSpec((tm, tn), lambda i, j: (i, j)),
        out_shape=jax.ShapeDtypeStruct((t, n), out_dtype),
        scratch_shapes=[pltpu.VMEM((tm, k), BF16)],
        compiler_params=_params("parallel", "arbitrary"),
        name=name,
    )(x, gain.reshape(1, k), w, *extras)


def _plain_epilogue(acc, j, o_ref):
    o_ref[...] = acc.astype(o_ref.dtype)


def _gelu_epilogue(acc, j, o_ref):
    c = 0.7978845608028654
    cdf = 0.5 * (1.0 + jnp.tanh(c * (acc + 0.044715 * (acc * acc * acc))))
    o_ref[...] = (acc * cdf).astype(o_ref.dtype)


def _headnorm_epilogue(acc, j, o_ref, hg_ref, *, n_norm_tiles):
    tn = acc.shape[1]

    @pl.when(j < n_norm_tiles)
    def _():
        g = hg_ref[...]
        for hh in range(tn // HEAD_DIM):
            sl = slice(hh * HEAD_DIM, (hh + 1) * HEAD_DIM)
            o_ref[:, sl] = _rms(acc[:, sl], g[:, sl]).astype(o_ref.dtype)

    @pl.when(j >= n_norm_tiles)
    def _():
        o_ref[...] = acc.astype(o_ref.dtype)


def _rope(x, cos_t, sin_t):
    lane = lax.broadcasted_iota(jnp.int32, x.shape, 1)
    half = MLA_ROPE // 2
    partner = jnp.where(lane < half, pltpu.roll(x, LANES - half, axis=1),
                        pltpu.roll(x, half, axis=1))
    return x * cos_t + partner * sin_t


def _rms_rope(x, gain, cos_t, sin_t):
    ms = jnp.sum(x * x, axis=-1, keepdims=True) * (1.0 / MLA_ROPE)
    return _rope((x * lax.rsqrt(ms + EPS)) * gain, cos_t, sin_t)


def _mla_q_epilogue(acc, j, o_ref, hg_ref, cos_ref, sin_ref, *, scale):
    tn = acc.shape[1]
    g = hg_ref[...]
    cos_t = cos_ref[...]
    sin_t = sin_ref[...]
    for hh in range(tn // (2 * LANES)):
        a = slice(hh * 2 * LANES, hh * 2 * LANES + LANES)
        b = slice(hh * 2 * LANES + LANES, (hh + 1) * 2 * LANES)
        o_ref[:, a] = (_rms(acc[:, a], g[:, a]) * scale).astype(o_ref.dtype)
        o_ref[:, b] = (_rms_rope(acc[:, b], g[:, b], cos_t, sin_t) * scale).astype(o_ref.dtype)


def _krope_kernel(y_ref, g_ref, cos_ref, sin_ref, o_ref):
    o_ref[...] = _rms_rope(y_ref[...], g_ref[...], cos_ref[...], sin_ref[...]).astype(o_ref.dtype)


def _rope_table_kernel(pos_ref, inv_ref, cmask_ref, ssign_ref, cos_ref, sin_ref):
    ang = pos_ref[...].astype(F32) * inv_ref[...]
    cos_ref[...] = jnp.cos(ang) * cmask_ref[...]
    sin_ref[...] = jnp.sin(ang) * ssign_ref[...]


def _proj_res_kernel(x_ref, w_ref, r_ref, o_ref):
    o_ref[...] = r_ref[...] + jnp.dot(x_ref[...], w_ref[...], preferred_element_type=F32)


def _proj_residual(x, w, res, *, tm=512, tn=2048, name):
    t, k = x.shape
    n = w.shape[1]
    tm = _pick(t, tm)
    tn = _pick(n, tn)
    return pl.pallas_call(
        _proj_res_kernel,
        grid=(t // tm, n // tn),
        in_specs=[
            pl.BlockSpec((tm, k), lambda i, j: (i, 0)),
            pl.BlockSpec((k, tn), lambda i, j: (0, j)),
            pl.BlockSpec((tm, tn), lambda i, j: (i, j)),
        ],
        out_specs=pl.BlockSpec((tm, tn), lambda i, j: (i, j)),
        out_shape=jax.ShapeDtypeStruct((t, n), F32),
        compiler_params=_params("parallel", "parallel"),
        name=name,
    )(x, w, res)


def _ffn_jobs(s, n_jobs, nf):
    job_a = jnp.minimum(s, n_jobs - 1)
    job_b = jnp.maximum(s - 1, 0)
    return job_a // nf, job_a % nf, job_b // nf, job_b % nf


def _ffn_kernel(h_ref, g_ref, wg_ref, wv_ref, cwg_ref, cwv_ref, cbg_ref, cbv_ref, wd_ref,
                o_ref, xn_ref, u0_ref, u1_ref, act_ref, carry_ref, *, nf, n_jobs, tiles_per_seq):
    s = pl.program_id(0)
    tm = o_ref.shape[0]
    tf = wg_ref.shape[1]
    _, ca, tb, cb = _ffn_jobs(s, n_jobs, nf)
    pad = SUBLANES
    sub = min(tf, 2 * LANES)

    @pl.when(s == 0)
    def _():
        u1_ref[...] = jnp.zeros(u1_ref.shape, F32)

    @pl.when(ca == 0)
    def _():
        xn_ref[...] = _rms(h_ref[...], g_ref[...]).astype(BF16)

    @pl.when(cb == 0)
    def _():
        o_ref[...] = h_ref[...]

    @pl.when(tb % tiles_per_seq == 0)
    def _():
        carry_ref[cb] = jnp.zeros(carry_ref.shape[1:], F32)

    def conv(u_ref, cols, taps, bias):
        y = bias + taps[0:1] * u_ref[pad - 2:pad - 2 + tm, cols]
        y = y + taps[1:2] * u_ref[pad - 1:pad - 1 + tm, cols]
        y = y + taps[2:3] * u_ref[pad:pad + tm, cols]
        return y

    def step(u_write, u_read):
        xn = xn_ref[...]
        u_write[pad:pad + tm, 0:tf] = jnp.dot(xn, wg_ref[...], preferred_element_type=F32)
        u_write[pad:pad + tm, tf:2 * tf] = jnp.dot(xn, wv_ref[...], preferred_element_type=F32)
        u_read[0:pad, :] = carry_ref[cb]
        for c in range(tf // sub):
            gs = slice(c * sub, (c + 1) * sub)
            vs = slice(tf + c * sub, tf + (c + 1) * sub)
            gate = conv(u_read, gs, cwg_ref[:, gs], cbg_ref[:, gs])
            val = conv(u_read, vs, cwv_ref[:, gs], cbv_ref[:, gs])
            act_ref[:, gs] = ((gate * (1.0 / (1.0 + jnp.exp(-gate)))) * val).astype(BF16)
            o_ref[...] += jnp.dot(act_ref[:, gs], wd_ref[gs, :], preferred_element_type=F32)
        carry_ref[cb] = u_read[tm:tm + pad, :]

    @pl.when(s % 2 == 0)
    def _():
        step(u0_ref, u1_ref)

    @pl.when(s % 2 == 1)
    def _():
        step(u1_ref, u0_ref)


def _ffn(h, gain, w_up, conv_w, conv_b, w_down, *, layer, seq, tm=512, tf=512):
    t, d = h.shape
    d_ff = w_down.shape[1]
    tm = _pick(seq, tm)
    tf = _pick(d_ff, tf)
    nf = d_ff // tf
    assert nf >= 2, "the output tile is initialised from the h block one step into a tile"
    n_jobs = (t // tm) * nf
    kern = functools.partial(_ffn_kernel, nf=nf, n_jobs=n_jobs, tiles_per_seq=seq // tm)
    jobs = functools.partial(_ffn_jobs, n_jobs=n_jobs, nf=nf)
    u_shape = (SUBLANES + tm, 2 * tf)
    return pl.pallas_call(
        kern,
        grid=(n_jobs + 1,),
        in_specs=[
            pl.BlockSpec((tm, d), lambda s: (jobs(s)[0], 0)),
            pl.BlockSpec((1, d), lambda s: (0, 0)),
            pl.BlockSpec((None, d, tf), lambda s: (layer, 0, jobs(s)[1])),
            pl.BlockSpec((None, d, tf), lambda s: (layer, 0, nf + jobs(s)[1])),
            pl.BlockSpec((CONV_WIDTH, tf), lambda s: (0, jobs(s)[3])),
            pl.BlockSpec((CONV_WIDTH, tf), lambda s: (0, nf + jobs(s)[3])),
            pl.BlockSpec((1, tf), lambda s: (0, jobs(s)[3])),
            pl.BlockSpec((1, tf), lambda s: (0, nf + jobs(s)[3])),
            pl.BlockSpec((None, tf, d), lambda s: (layer, jobs(s)[3], 0)),
        ],
        out_specs=pl.BlockSpec((tm, d), lambda s: (jobs(s)[2], 0)),
        out_shape=jax.ShapeDtypeStruct((t, d), F32),
        scratch_shapes=[pltpu.VMEM((tm, d), BF16), pltpu.VMEM(u_shape, F32),
                        pltpu.VMEM(u_shape, F32), pltpu.VMEM((tm, tf), BF16),
                        pltpu.VMEM((nf, SUBLANES, 2 * tf), F32)],
        compiler_params=_params("arbitrary"),
        name="conv_ffn",
    )(h, gain.reshape(1, d), w_up, w_up, conv_w, conv_w, conv_b.reshape(1, -1),
      conv_b.reshape(1, -1), w_down)


def _fox_cum_kernel(f_ref, b_ref, o_ref, *, chunk):
    s = f_ref.shape[0]
    r = lax.broadcasted_iota(jnp.int32, (chunk, chunk), 0)
    c = lax.broadcasted_iota(jnp.int32, (chunk, chunk), 1)
    tri = (r <= c).astype(BF16)
    carry = jnp.zeros((LANES, 1), F32)
    for ci in range(s // chunk):
        x = f_ref[ci * chunk:(ci + 1) * chunk, :] + b_ref[...]
        lf = jnp.minimum(x, 0.0) - jnp.log1p(jnp.exp(-jnp.abs(x)))
        lft = lf.T
        hi = lft.astype(BF16)
        r1 = lft - hi.astype(F32)
        mid = r1.astype(BF16)
        lo = (r1 - mid.astype(F32)).astype(BF16)
        local = (jnp.dot(hi, tri, preferred_element_type=F32)
                 + jnp.dot(mid, tri, preferred_element_type=F32)
                 + jnp.dot(lo, tri, preferred_element_type=F32)) + carry
        o_ref[:, ci * chunk:(ci + 1) * chunk] = local[:N_HEADS]
        carry = local[:, chunk - 1:chunk]


def _fox_cum(f_logit, b_f, *, batch, seq):
    chunk = _pick(seq, 512)
    return pl.pallas_call(
        functools.partial(_fox_cum_kernel, chunk=chunk),
        grid=(batch,),
        in_specs=[pl.BlockSpec((seq, LANES), lambda b: (b, 0)),
                  pl.BlockSpec((1, LANES), lambda b: (0, 0))],
        out_specs=pl.BlockSpec((None, N_HEADS, seq), lambda b: (b, 0, 0)),
        out_shape=jax.ShapeDtypeStruct((batch, N_HEADS, seq), F32),
        compiler_params=_params("parallel"),
        name="fox_cum",
    )(f_logit, b_f)


def _softmax_attn_kernel(*refs, tq, mode):
    if mode == "fox":
        q_ref, k_ref, v_ref, c_ref, o_ref = refs
    else:
        q_ref, k_ref, kr_ref, v_ref, o_ref = refs
    h = pl.program_id(1)
    seq = q_ref.shape[0]
    row = lax.broadcasted_iota(jnp.int32, (tq, tq), 0)
    col = lax.broadcasted_iota(jnp.int32, (tq, tq), 1)
    allowed = col <= row

    def blk(i):
        return slice(i * tq, (i + 1) * tq)

    tiles = [(qi, kj) for qi in range(seq // tq) for kj in range(qi + 1)]
    scores = {}
    m = l = acc = None
    for n in range(len(tiles) + 1):
        if n < len(tiles):
            qi, kj = tiles[n]
            k = k_ref[blk(kj), :]
            if mode == "mla":
                k = jnp.concatenate([k, kr_ref[blk(kj), :]], axis=1)
            scores[n] = lax.dot_general(q_ref[blk(qi), :], k, (((1,), (1,)), ((), ())),
                                        preferred_element_type=F32)
        if n >= 1:
            qi, kj = tiles[n - 1]
            s = scores.pop(n - 1)
            if kj == 0:
                m = jnp.full((tq, 1), -jnp.inf, F32)
                l = jnp.zeros((tq, 1), F32)
                acc = jnp.zeros((tq, HEAD_DIM), F32)
            if mode == "fox":
                s = s - c_ref[pl.ds(h, 1), blk(kj)]
            if kj == qi:
                s = jnp.where(allowed, s, -jnp.inf)
            m_new = jnp.maximum(m, jnp.max(s, axis=1, keepdims=True))
            alpha = jnp.exp(m - m_new)
            p = jnp.exp(s - m_new)
            l = alpha * l + jnp.sum(p, axis=1, keepdims=True)
            acc = alpha * acc + jnp.dot(p.astype(BF16), v_ref[blk(kj), :],
                                        preferred_element_type=F32)
            m = m_new
            if kj == qi:
                o_ref[blk(qi), :] = (acc / l).astype(o_ref.dtype)


def _fox_attention(qkv, cum, *, batch, seq, tq=512):
    t = qkv.shape[0]
    tq = _pick(seq, tq)
    blk = lambda off: pl.BlockSpec((seq, HEAD_DIM), lambda b, h: (b, off + h))
    return pl.pallas_call(
        functools.partial(_softmax_attn_kernel, tq=tq, mode="fox"),
        grid=(batch, N_HEADS),
        in_specs=[blk(0), blk(N_HEADS), blk(2 * N_HEADS),
                  pl.BlockSpec((None, N_HEADS, seq), lambda b, h: (b, 0, 0))],
        out_specs=blk(0),
        out_shape=jax.ShapeDtypeStruct((t, N_HEADS * HEAD_DIM), BF16),
        compiler_params=_params("parallel", "parallel"),
        name="fox_attention",
    )(qkv, qkv, qkv, cum)


def _mla_attention(q, kv, k_rope, *, batch, seq, tq=512):
    t = q.shape[0]
    tq = _pick(seq, tq)
    return pl.pallas_call(
        functools.partial(_softmax_attn_kernel, tq=tq, mode="mla"),
        grid=(batch, N_HEADS),
        in_specs=[pl.BlockSpec((seq, 2 * LANES), lambda b, h: (b, h)),
                  pl.BlockSpec((seq, HEAD_DIM), lambda b, h: (b, h)),
                  pl.BlockSpec((seq, LANES), lambda b, h: (b, 0)),
                  pl.BlockSpec((seq, HEAD_DIM), lambda b, h: (b, N_HEADS + h))],
        out_specs=pl.BlockSpec((seq, HEAD_DIM), lambda b, h: (b, h)),
        out_shape=jax.ShapeDtypeStruct((t, N_HEADS * HEAD_DIM), BF16),
        compiler_params=_params("parallel", "parallel"),
        name="mla_attention",
    )(q, kv, k_rope, kv)


def _sb_attn_kernel(q_ref, k_ref, v_ref, o_ref, *, tq):
    seq = q_ref.shape[0]
    row = lax.broadcasted_iota(jnp.int32, (tq, tq), 0)
    col = lax.broadcasted_iota(jnp.int32, (tq, tq), 1)
    strict = col < row
    neg_after = jnp.where(row > col, -1.0, 0.0).astype(BF16)
    neg_after2 = jnp.concatenate([neg_after, neg_after], axis=0)

    def blk(i):
        return slice(i * tq, (i + 1) * tq)

    tiles = [(qi, kj) for qi in range(seq // tq) for kj in range(qi, -1, -1)]
    scores, mids = {}, {}
    acc = later = None
    for n in range(len(tiles) + 2):
        if n < len(tiles):
            qi, kj = tiles[n]
            scores[n] = lax.dot_general(q_ref[blk(qi), :], k_ref[blk(kj), :],
                                        (((1,), (1,)), ((), ())), preferred_element_type=F32)
        if 0 <= n - 1 < len(tiles):
            qi, kj = tiles[n - 1]
            z = scores.pop(n - 1)
            sp = jnp.maximum(z, 0.0) + jnp.log(1.0 + jnp.exp2(jnp.abs(z) * -LOG2_E))
            if kj == qi:
                sp = jnp.where(strict, sp, 0.0)
            hi = sp.astype(BF16)
            lo = (sp - hi.astype(F32)).astype(BF16)
            after = jnp.dot(jnp.concatenate([hi, lo], axis=1), neg_after2,
                            preferred_element_type=F32)
            mids[n - 1] = (z - sp, after, sp[:, 0:1])
        if 0 <= n - 2 < len(tiles):
            qi, kj = tiles[n - 2]
            log_beta, after, sp0 = mids.pop(n - 2)
            if kj == qi:
                acc = jnp.zeros((tq, HEAD_DIM), F32)
                later = jnp.zeros((tq, 1), F32)
            a = jnp.exp((log_beta + after) + later)
            if kj == qi:
                a = jnp.where(strict, a, 0.0)
            acc = acc + jnp.dot(a.astype(BF16), v_ref[blk(kj), :], preferred_element_type=F32)
            later = later + (after[:, 0:1] - sp0)
            if kj == 0:
                o_ref[blk(qi), :] = acc.astype(o_ref.dtype)


def _sb_attention(qkv, *, batch, seq, tq=256):
    t = qkv.shape[0]
    tq = _pick(seq, tq)
    blk = lambda off: pl.BlockSpec((seq, HEAD_DIM), lambda b, h: (b, off + h))
    return pl.pallas_call(
        functools.partial(_sb_attn_kernel, tq=tq),
        grid=(batch, N_HEADS),
        in_specs=[blk(0), blk(N_HEADS), blk(2 * N_HEADS)],
        out_specs=blk(0),
        out_shape=jax.ShapeDtypeStruct((t, N_HEADS * HEAD_DIM), BF16),
        compiler_params=_params("parallel", "parallel"),
        name="sb_attention",
    )(qkv, qkv, qkv)


def _sgu_kernel(u_ref, v_ref, g_ref, ws_ref, bs_ref, o_ref):
    tm, width = v_ref.shape
    row = lax.broadcasted_iota(jnp.int32, (SGU_CHUNK, SGU_CHUNK), 0)
    col = lax.broadcasted_iota(jnp.int32, (SGU_CHUNK, SGU_CHUNK), 1)
    causal = col <= row
    vn = _rms(v_ref[...].astype(F32), g_ref[...]).astype(BF16)
    bs = bs_ref[...]
    for g in range(width // LANES):
        cs = slice(g * LANES, (g + 1) * LANES)
        w = jnp.where(causal, ws_ref[g], 0.0).astype(BF16)
        for n in range(tm // SGU_CHUNK):
            rs = slice(n * SGU_CHUNK, (n + 1) * SGU_CHUNK)
            mixed = jnp.dot(w, vn[rs, cs], preferred_element_type=F32) + bs[:, g:g + 1]
            o_ref[rs, cs] = (u_ref[rs, cs].astype(F32) * mixed).astype(o_ref.dtype)


def _sgu_spatial(uv, v_gain, w_s, b_s, *, tm=512):
    t = uv.shape[0]
    width = uv.shape[1] // 2
    tm = _pick(t, tm)
    return pl.pallas_call(
        _sgu_kernel,
        grid=(t // tm,),
        in_specs=[pl.BlockSpec((tm, width), lambda i: (i, 0)),
                  pl.BlockSpec((tm, width), lambda i: (i, 1)),
                  pl.BlockSpec((1, width), lambda i: (0, 0)),
                  pl.BlockSpec(w_s.shape, lambda i: (0, 0, 0)),
                  pl.BlockSpec((SGU_CHUNK, SGU_GROUPS), lambda i: (0, 0))],
        out_specs=pl.BlockSpec((tm, width), lambda i: (i, 0)),
        out_shape=jax.ShapeDtypeStruct((t, width), BF16),
        compiler_params=_params("parallel"),
        name="sgu_spatial",
    )(uv, uv, v_gain.reshape(1, width), w_s, b_s.T)


def _tile_gain(gain, reps):
    return jnp.tile(gain.astype(F32), reps)


def _qkv_mixer_proj(h, norm_gain, w_qkv, q_gain, k_gain, scale, name):
    hd = N_HEADS * HEAD_DIM
    tn = 1024
    per = tn // HEAD_DIM
    gains = jnp.stack([_tile_gain(q_gain * scale, per), _tile_gain(k_gain, per),
                       jnp.ones((tn,), F32)]).reshape(3, 1, tn)
    tiles_per_seg = hd // tn
    epi = functools.partial(_headnorm_epilogue, n_norm_tiles=2 * tiles_per_seg)
    return _norm_proj(
        h, norm_gain, w_qkv.astype(BF16), epi, BF16, k=h.shape[1], tn=tn,
        extras=(gains,),
        extra_specs=(pl.BlockSpec((None, 1, tn), lambda i, j: (j // tiles_per_seg, 0, 0)),),
        name=name)


def _fox_mixer(h, norm_gain, w_in, b_f, q_gain, k_gain, w_out, *, batch, seq):
    hd = N_HEADS * HEAD_DIM
    d = h.shape[1]
    qkv = _qkv_mixer_proj(h, norm_gain, w_in[:, :3 * hd], q_gain, k_gain, HEAD_DIM ** -0.5,
                          "fox_qkv_proj")
    w_f = jnp.pad(w_in[:, 3 * hd:], ((0, 0), (0, LANES - N_HEADS))).astype(BF16)
    f_logit = _norm_proj(h, norm_gain, w_f, _plain_epilogue, F32, k=d, tn=LANES,
                         name="fox_fgate_proj")
    b_pad = jnp.pad(b_f.astype(F32), (0, LANES - N_HEADS)).reshape(1, LANES)
    cum = _fox_cum(f_logit, b_pad, batch=batch, seq=seq)
    o = _fox_attention(qkv, cum, batch=batch, seq=seq)
    return _proj_residual(o, w_out.astype(BF16), h, name="fox_out_proj")


def _mla_mixer(h, positions, norm_gain, w_in, q_a_gain, kv_a_gain, w_q_b, w_kv_b, q_gain,
               k_gain, w_out, *, batch, seq):
    t, d = h.shape
    n_in = MLA_Q_RANK + MLA_KV_RANK + MLA_ROPE
    n_in_pad = MLA_Q_RANK + MLA_KV_RANK + LANES
    w_in_p = jnp.pad(w_in, ((0, 0), (0, n_in_pad - n_in))).astype(BF16)
    y = _norm_proj(h, norm_gain, w_in_p, _plain_epilogue, F32, k=d, tn=n_in_pad,
                   name="mla_down_proj")

    half = MLA_ROPE // 2
    inv_freq = ROPE_THETA ** (-jnp.arange(0, MLA_ROPE, 2, dtype=F32) / MLA_ROPE)
    zeros = jnp.zeros((LANES - MLA_ROPE,), F32)
    inv_row = jnp.concatenate([inv_freq, inv_freq, zeros]).reshape(1, LANES)
    cmask = jnp.concatenate([jnp.ones((MLA_ROPE,), F32), zeros]).reshape(1, LANES)
    ssign = jnp.concatenate([-jnp.ones((half,), F32), jnp.ones((half,), F32), zeros]).reshape(1, LANES)
    tm_small = _pick(t, 1024)
    row_spec = pl.BlockSpec((tm_small, LANES), lambda i: (i, 0))
    const_spec = pl.BlockSpec((1, LANES), lambda i: (0, 0))
    cos_t, sin_t = pl.pallas_call(
        _rope_table_kernel,
        grid=(t // tm_small,),
        in_specs=[pl.BlockSpec((tm_small, 1), lambda i: (i, 0)), const_spec, const_spec, const_spec],
        out_specs=[row_spec, row_spec],
        out_shape=[jax.ShapeDtypeStruct((t, LANES), F32)] * 2,
        compiler_params=_params("parallel"),
        name="rope_tables",
    )(positions.reshape(t, 1), inv_row, cmask, ssign)

    scale = (MLA_NOPE + MLA_ROPE) ** -0.5
    wq = w_q_b.reshape(MLA_Q_RANK, N_HEADS, MLA_NOPE + MLA_ROPE)
    wq = jnp.pad(wq, ((0, 0), (0, 0), (0, 2 * LANES - MLA_NOPE - MLA_ROPE)))
    wq = wq.reshape(MLA_Q_RANK, N_HEADS * 2 * LANES).astype(BF16)
    tn = 1024
    q_head_gain = jnp.concatenate([q_gain.astype(F32), zeros])
    q_gains = _tile_gain(q_head_gain, tn // (2 * LANES)).reshape(1, tn)
    tm = _pick(t, 1024)
    tab_spec = pl.BlockSpec((tm, LANES), lambda i, j: (i, 0))
    q = _norm_proj(y, q_a_gain, wq, functools.partial(_mla_q_epilogue, scale=scale), BF16,
                   k=MLA_Q_RANK, x_col_block=0, tm=tm, tn=tn,
                   extras=(q_gains, cos_t, sin_t),
                   extra_specs=(pl.BlockSpec((1, tn), lambda i, j: (0, 0)), tab_spec, tab_spec),
                   name="mla_q_proj")

    wkv = w_kv_b.reshape(MLA_KV_RANK, N_HEADS, MLA_NOPE + HEAD_DIM)
    wkv = jnp.concatenate([wkv[:, :, :MLA_NOPE].reshape(MLA_KV_RANK, -1),
                           wkv[:, :, MLA_NOPE:].reshape(MLA_KV_RANK, -1)], axis=1).astype(BF16)
    k_gains = jnp.stack([_tile_gain(k_gain[:MLA_NOPE], tn // HEAD_DIM),
                         jnp.ones((tn,), F32)]).reshape(2, 1, tn)
    tiles_per_seg = N_HEADS * HEAD_DIM // tn
    kv = _norm_proj(y, kv_a_gain, wkv,
                    functools.partial(_headnorm_epilogue, n_norm_tiles=tiles_per_seg), BF16,
                    k=MLA_KV_RANK, x_col_block=1, tn=tn,
                    extras=(k_gains,),
                    extra_specs=(pl.BlockSpec((None, 1, tn), lambda i, j: (j // tiles_per_seg, 0, 0)),),
                    name="mla_kv_proj")

    kr_gain = jnp.concatenate([k_gain[MLA_NOPE:].astype(F32), zeros]).reshape(1, LANES)
    k_rope = pl.pallas_call(
        _krope_kernel,
        grid=(t // tm_small,),
        in_specs=[pl.BlockSpec((tm_small, LANES), lambda i: (i, (MLA_Q_RANK + MLA_KV_RANK) // LANES)),
                  const_spec, row_spec, row_spec],
        out_specs=row_spec,
        out_shape=jax.ShapeDtypeStruct((t, LANES), BF16),
        compiler_params=_params("parallel"),
        name="mla_k_rope",
    )(y, kr_gain, cos_t, sin_t)

    o = _mla_attention(q, kv, k_rope, batch=batch, seq=seq)
    return _proj_residual(o, w_out.astype(BF16), h, name="mla_out_proj")


def _sb_mixer(h, norm_gain, w_in, q_gain, k_gain, w_out, *, batch, seq):
    qkv = _qkv_mixer_proj(h, norm_gain, w_in, q_gain, k_gain, HEAD_DIM ** -0.5, "sb_qkv_proj")
    o = _sb_attention(qkv, batch=batch, seq=seq)
    return _proj_residual(o, w_out.astype(BF16), h, name="sb_out_proj")


def _sgu_mixer(h, norm_gain, w_in, v_gain, w_s, b_s, w_out):
    uv = _norm_proj(h, norm_gain, w_in.astype(BF16), _gelu_epilogue, BF16, k=h.shape[1],
                    name="sgu_in_proj")
    gated = _sgu_spatial(uv, v_gain, w_s, b_s)
    return _proj_residual(gated, w_out.astype(BF16), h, name="sgu_out_proj")


def kernel(x, positions, mix_norm, ffn_norm, fox_w_in, fox_b_f, fox_q_gain, fox_k_gain, fox_w_out, mla_w_in, mla_q_a_gain, mla_kv_a_gain, mla_w_q_b, mla_w_kv_b, mla_q_gain, mla_k_gain, mla_w_out, sb_w_in, sb_q_gain, sb_k_gain, sb_w_out, sgu_w_in, sgu_v_gain, sgu_w_s, sgu_b_s, sgu_w_out, ffn_w_up, ffn_conv_w, ffn_conv_b, ffn_w_down):
    batch, seq, d = x.shape
    depth = mix_norm.shape[0]
    n_mixers = 4
    h = x.reshape(batch * seq, d)
    w_up_all = ffn_w_up.astype(BF16)
    w_down_all = ffn_w_down.astype(BF16)
    for i in range(depth):
        m, j = i % n_mixers, i // n_mixers
        if m == 0:
            h = _fox_mixer(h, mix_norm[i], fox_w_in[j], fox_b_f[j], fox_q_gain[j], fox_k_gain[j],
                           fox_w_out[j], batch=batch, seq=seq)
        elif m == 1:
            h = _mla_mixer(h, positions, mix_norm[i], mla_w_in[j], mla_q_a_gain[j],
                           mla_kv_a_gain[j], mla_w_q_b[j], mla_w_kv_b[j], mla_q_gain[j],
                           mla_k_gain[j], mla_w_out[j], batch=batch, seq=seq)
        elif m == 2:
            h = _sb_mixer(h, mix_norm[i], sb_w_in[j], sb_q_gain[j], sb_k_gain[j], sb_w_out[j],
                          batch=batch, seq=seq)
        else:
            h = _sgu_mixer(h, mix_norm[i], sgu_w_in[j], sgu_v_gain[j], sgu_w_s[j], sgu_b_s[j],
                           sgu_w_out[j])
        h = _ffn(h, ffn_norm[i], w_up_all, ffn_conv_w[i], ffn_conv_b[i], w_down_all,
                 layer=i, seq=seq)
    return h.reshape(batch, seq, d)
```

```python
import functools

import jax
import jax.numpy as jnp
from jax import lax
from jax.experimental import pallas as pl
from jax.experimental.pallas import tpu as pltpu

F32 = jnp.float32
BF16 = jnp.bfloat16

EPS = 1e-6
N_HEADS = 16
HEAD_DIM = 128
LANES = 128
SUBLANES = 8
MLA_Q_RANK = 512
MLA_KV_RANK = 512
MLA_NOPE = 128
MLA_ROPE = 64
ROPE_THETA = 10000.0
SGU_CHUNK = 128
SGU_GROUPS = 16
CONV_WIDTH = 3
LOG2_E = 1.4426950408889634
VMEM_LIMIT_BYTES = 56 * 1024 * 1024


def _params(*sem, flags=None):
    return pltpu.CompilerParams(dimension_semantics=sem, vmem_limit_bytes=VMEM_LIMIT_BYTES,
                                flags=flags)


def _pick(total, pref):
    t = min(pref, total)
    while total % t:
        t //= 2
    return t


def _rms(x, gain):
    ms = jnp.mean(x * x, axis=-1, keepdims=True)
    return (x * lax.rsqrt(ms + EPS)) * gain


def _norm_proj_kernel(x_ref, g_ref, w_ref, *rest, epilogue, n_extra):
    extra = rest[:n_extra]
    o_ref = rest[n_extra]
    xn_ref = rest[n_extra + 1]
    j = pl.program_id(1)

    @pl.when(j == 0)
    def _():
        xn_ref[...] = _rms(x_ref[...], g_ref[...]).astype(BF16)

    acc = jnp.dot(xn_ref[...], w_ref[...], preferred_element_type=F32)
    epilogue(acc, j, o_ref, *extra)


def _norm_proj(x, gain, w, epilogue, out_dtype, *, k, x_col_block=0, tm=1024, tn=1024,
               extras=(), extra_specs=(), name):
    t = x.shape[0]
    n = w.shape[1]
    tm = _pick(t, tm)
    tn = _pick(n, tn)
    kern = functools.partial(_norm_proj_kernel, epilogue=epilogue, n_extra=len(extras))
    return pl.pallas_call(
        kern,
        grid=(t // tm, n // tn),
        in_specs=[
            pl.BlockSpec((tm, k), lambda i, j: (i, x_col_block)),
            pl.BlockSpec((1, k), lambda i, j: (0, 0)),
            pl.BlockSpec((k, tn), lambda i, j: (0, j)),
            *extra_specs,
        ],
        out_specs=pl.BlockSpec((tm, tn), lambda i, j: (i, j)),
        out_shape=jax.ShapeDtypeStruct((t, n), out_dtype),
        scratch_shapes=[pltpu.VMEM((tm, k), BF16)],
        compiler_params=_params("parallel", "arbitrary"),
        name=name,
    )(x, gain.reshape(1, k), w, *extras)


def _plain_epilogue(acc, j, o_ref):
    o_ref[...] = acc.astype(o_ref.dtype)


def _gelu_epilogue(acc, j, o_ref):
    c = 0.7978845608028654
    cdf = 0.5 * (1.0 + jnp.tanh(c * (acc + 0.044715 * (acc * acc * acc))))
    o_ref[...] = (acc * cdf).astype(o_ref.dtype)


def _headnorm_epilogue(acc, j, o_ref, hg_ref, *, n_norm_tiles):
    tn = acc.shape[1]

    @pl.when(j < n_norm_tiles)
    def _():
        g = hg_ref[...]
        for hh in range(tn // HEAD_DIM):
            sl = slice(hh * HEAD_DIM, (hh + 1) * HEAD_DIM)
            o_ref[:, sl] = _rms(acc[:, sl], g[:, sl]).astype(o_ref.dtype)

    @pl.when(j >= n_norm_tiles)
    def _():
        o_ref[...] = acc.astype(o_ref.dtype)


def _rope(x, cos_t, sin_t):
    lane = lax.broadcasted_iota(jnp.int32, x.shape, 1)
    half = MLA_ROPE // 2
    partner = jnp.where(lane < half, pltpu.roll(x, LANES - half, axis=1),
                        pltpu.roll(x, half, axis=1))
    return x * cos_t + partner * sin_t


def _rms_rope(x, gain, cos_t, sin_t):
    ms = jnp.sum(x * x, axis=-1, keepdims=True) * (1.0 / MLA_ROPE)
    return _rope((x * lax.rsqrt(ms + EPS)) * gain, cos_t, sin_t)


def _mla_q_epilogue(acc, j, o_ref, hg_ref, cos_ref, sin_ref, *, scale):
    tn = acc.shape[1]
    g = hg_ref[...]
    cos_t = cos_ref[...]
    sin_t = sin_ref[...]
    for hh in range(tn // (2 * LANES)):
        a = slice(hh * 2 * LANES, hh * 2 * LANES + LANES)
        b = slice(hh * 2 * LANES + LANES, (hh + 1) * 2 * LANES)
        o_ref[:, a] = (_rms(acc[:, a], g[:, a]) * scale).astype(o_ref.dtype)
        o_ref[:, b] = (_rms_rope(acc[:, b], g[:, b], cos_t, sin_t) * scale).astype(o_ref.dtype)


def _krope_kernel(y_ref, g_ref, cos_ref, sin_ref, o_ref):
    o_ref[...] = _rms_rope(y_ref[...], g_ref[...], cos_ref[...], sin_ref[...]).astype(o_ref.dtype)


def _rope_table_kernel(pos_ref, inv_ref, cmask_ref, ssign_ref, cos_ref, sin_ref):
    ang = pos_ref[...].astype(F32) * inv_ref[...]
    cos_ref[...] = jnp.cos(ang) * cmask_ref[...]
    sin_ref[...] = jnp.sin(ang) * ssign_ref[...]


def _proj_res_kernel(x_ref, w_ref, r_ref, o_ref):
    o_ref[...] = r_ref[...] + jnp.dot(x_ref[...], w_ref[...], preferred_element_type=F32)


def _proj_residual(x, w, res, *, tm=512, tn=2048, name):
    t, k = x.shape
    n = w.shape[1]
    tm = _pick(t, tm)
    tn = _pick(n, tn)
    return pl.pallas_call(
        _proj_res_kernel,
        grid=(t // tm, n // tn),
        in_specs=[
            pl.BlockSpec((tm, k), lambda i, j: (i, 0)),
            pl.BlockSpec((k, tn), lambda i, j: (0, j)),
            pl.BlockSpec((tm, tn), lambda i, j: (i, j)),
        ],
        out_specs=pl.BlockSpec((tm, tn), lambda i, j: (i, j)),
        out_shape=jax.ShapeDtypeStruct((t, n), F32),
        compiler_params=_params("parallel", "parallel"),
        name=name,
    )(x, w, res)


def _ffn_jobs(s, n_jobs, nf):
    job_a = jnp.minimum(s, n_jobs - 1)
    job_b = jnp.maximum(s - 1, 0)
    return job_a // nf, job_a % nf, job_b // nf, job_b % nf


def _ffn_kernel(h_ref, g_ref, wg_ref, wv_ref, cwg_ref, cwv_ref, cbg_ref, cbv_ref, wd_ref,
                o_ref, xn_ref, act0_ref, act1_ref, carry_ref, *, nf, n_jobs, tiles_per_seq):
    s = pl.program_id(0)
    tf = wg_ref.shape[1]
    ta, ca, _, cb = _ffn_jobs(s, n_jobs, nf)

    @pl.when(s == 0)
    def _():
        act1_ref[...] = jnp.zeros(act1_ref.shape, BF16)

    @pl.when(ca == 0)
    def _():
        xn_ref[...] = _rms(h_ref[...], g_ref[...]).astype(BF16)

    @pl.when(cb == 0)
    def _():
        o_ref[...] = h_ref[...]

    @pl.when(ta % tiles_per_seq == 0)
    def _():
        carry_ref[ca] = jnp.zeros(carry_ref.shape[1:], F32)

    def conv(u, prev_part, taps, bias):
        r1 = pltpu.roll(u, 1, axis=0)
        r2 = pltpu.roll(u, 2, axis=0)
        row = lax.broadcasted_iota(jnp.int32, u.shape, 0)
        u1 = jnp.where(row == 0, prev_part[1:2], r1)
        u2 = jnp.where(row == 0, prev_part[0:1], jnp.where(row == 1, prev_part[1:2], r2))
        y = bias + taps[0:1] * u2
        y = y + taps[1:2] * u1
        y = y + taps[2:3] * u
        return y, r2[0:SUBLANES]

    def step(act_write, act_read):
        xn = xn_ref[...]
        ug = jnp.dot(xn, wg_ref[...], preferred_element_type=F32)
        uv = jnp.dot(xn, wv_ref[...], preferred_element_type=F32)
        prev = carry_ref[ca]
        gate, new_g = conv(ug, prev[:, 0:tf], cwg_ref[...], cbg_ref[...])
        val, new_v = conv(uv, prev[:, tf:2 * tf], cwv_ref[...], cbv_ref[...])
        carry_ref[ca, :, 0:tf] = new_g
        carry_ref[ca, :, tf:2 * tf] = new_v
        act_write[...] = ((gate * (1.0 / (1.0 + jnp.exp(-gate)))) * val).astype(BF16)
        o_ref[...] += jnp.dot(act_read[...], wd_ref[...], preferred_element_type=F32)

    @pl.when(s % 2 == 0)
    def _():
        step(act0_ref, act1_ref)

    @pl.when(s % 2 == 1)
    def _():
        step(act1_ref, act0_ref)


def _ffn(h, gain, w_up, conv_w, conv_b, w_down, *, layer, seq, tm=512, tf=512):
    t, d = h.shape
    d_ff = w_down.shape[1]
    tm = _pick(seq, tm)
    tf = _pick(d_ff, tf)
    nf = d_ff // tf
    assert nf >= 2, "the output tile is initialised from the h block one step into a tile"
    n_jobs = (t // tm) * nf
    kern = functools.partial(_ffn_kernel, nf=nf, n_jobs=n_jobs, tiles_per_seq=seq // tm)
    jobs = functools.partial(_ffn_jobs, n_jobs=n_jobs, nf=nf)
    return pl.pallas_call(
        kern,
        grid=(n_jobs + 1,),
        in_specs=[
            pl.BlockSpec((tm, d), lambda s: (jobs(s)[0], 0)),
            pl.BlockSpec((1, d), lambda s: (0, 0)),
            pl.BlockSpec((None, d, tf), lambda s: (layer, 0, jobs(s)[1])),
            pl.BlockSpec((None, d, tf), lambda s: (layer, 0, nf + jobs(s)[1])),
            pl.BlockSpec((CONV_WIDTH, tf), lambda s: (0, jobs(s)[1])),
            pl.BlockSpec((CONV_WIDTH, tf), lambda s: (0, nf + jobs(s)[1])),
            pl.BlockSpec((1, tf), lambda s: (0, jobs(s)[1])),
            pl.BlockSpec((1, tf), lambda s: (0, nf + jobs(s)[1])),
            pl.BlockSpec((None, tf, d), lambda s: (layer, jobs(s)[3], 0)),
        ],
        out_specs=pl.BlockSpec((tm, d), lambda s: (jobs(s)[2], 0)),
        out_shape=jax.ShapeDtypeStruct((t, d), F32),
        scratch_shapes=[pltpu.VMEM((tm, d), BF16), pltpu.VMEM((tm, tf), BF16),
                        pltpu.VMEM((tm, tf), BF16), pltpu.VMEM((nf, SUBLANES, 2 * tf), F32)],
        compiler_params=_params("arbitrary"),
        name="conv_ffn",
    )(h, gain.reshape(1, d), w_up, w_up, conv_w, conv_w, conv_b.reshape(1, -1),
      conv_b.reshape(1, -1), w_down)


def _fox_cum_kernel(f_ref, b_ref, o_ref, *, chunk):
    s = f_ref.shape[0]
    r = lax.broadcasted_iota(jnp.int32, (chunk, chunk), 0)
    c = lax.broadcasted_iota(jnp.int32, (chunk, chunk), 1)
    tri = (r <= c).astype(BF16)
    carry = jnp.zeros((LANES, 1), F32)
    for ci in range(s // chunk):
        x = f_ref[ci * chunk:(ci + 1) * chunk, :] + b_ref[...]
        lf = jnp.minimum(x, 0.0) - jnp.log1p(jnp.exp(-jnp.abs(x)))
        lft = lf.T
        hi = lft.astype(BF16)
        r1 = lft - hi.astype(F32)
        mid = r1.astype(BF16)
        lo = (r1 - mid.astype(F32)).astype(BF16)
        local = (jnp.dot(hi, tri, preferred_element_type=F32)
                 + jnp.dot(mid, tri, preferred_element_type=F32)
                 + jnp.dot(lo, tri, preferred_element_type=F32)) + carry
        o_ref[:, ci * chunk:(ci + 1) * chunk] = local[:N_HEADS]
        carry = local[:, chunk - 1:chunk]


def _fox_cum(f_logit, b_f, *, batch, seq):
    chunk = _pick(seq, 512)
    return pl.pallas_call(
        functools.partial(_fox_cum_kernel, chunk=chunk),
        grid=(batch,),
        in_specs=[pl.BlockSpec((seq, LANES), lambda b: (b, 0)),
                  pl.BlockSpec((1, LANES), lambda b: (0, 0))],
        out_specs=pl.BlockSpec((None, N_HEADS, seq), lambda b: (b, 0, 0)),
        out_shape=jax.ShapeDtypeStruct((batch, N_HEADS, seq), F32),
        compiler_params=_params("parallel"),
        name="fox_cum",
    )(f_logit, b_f)


def _softmax_attn_kernel(*refs, tq, mode):
    if mode == "fox":
        q_ref, k_ref, v_ref, c_ref, o_ref = refs
    else:
        q_ref, k_ref, kr_ref, v_ref, o_ref = refs
    h = pl.program_id(1)
    seq = q_ref.shape[0]
    row = lax.broadcasted_iota(jnp.int32, (tq, tq), 0)
    col = lax.broadcasted_iota(jnp.int32, (tq, tq), 1)
    allowed = col <= row

    def blk(i):
        return slice(i * tq, (i + 1) * tq)

    tiles = [(qi, kj) for qi in range(seq // tq) for kj in range(qi + 1)]
    scores = {}
    m = l = acc = None
    for n in range(len(tiles) + 1):
        if n < len(tiles):
            qi, kj = tiles[n]
            k = k_ref[blk(kj), :]
            if mode == "mla":
                k = jnp.concatenate([k, kr_ref[blk(kj), :]], axis=1)
            scores[n] = lax.dot_general(q_ref[blk(qi), :], k, (((1,), (1,)), ((), ())),
                                        preferred_element_type=F32)
        if n >= 1:
            qi, kj = tiles[n - 1]
            s = scores.pop(n - 1)
            if kj == 0:
                m = jnp.full((tq, 1), -jnp.inf, F32)
                l = jnp.zeros((tq, 1), F32)
                acc = jnp.zeros((tq, HEAD_DIM), F32)
            if mode == "fox":
                s = s - c_ref[pl.ds(h, 1), blk(kj)]
            if kj == qi:
                s = jnp.where(allowed, s, -jnp.inf)
            m_new = jnp.maximum(m, jnp.max(s, axis=1, keepdims=True))
            alpha = jnp.exp(m - m_new)
            p = jnp.exp(s - m_new)
            l = alpha * l + jnp.sum(p, axis=1, keepdims=True)
            acc = alpha * acc + jnp.dot(p.astype(BF16), v_ref[blk(kj), :],
                                        preferred_element_type=F32)
            m = m_new
            if kj == qi:
                o_ref[blk(qi), :] = (acc / l).astype(o_ref.dtype)


def _fox_attention(qkv, cum, *, batch, seq, tq=512):
    t = qkv.shape[0]
    tq = _pick(seq, tq)
    blk = lambda off: pl.BlockSpec((seq, HEAD_DIM), lambda b, h: (b, off + h))
    return pl.pallas_call(
        functools.partial(_softmax_attn_kernel, tq=tq, mode="fox"),
        grid=(batch, N_HEADS),
        in_specs=[blk(0), blk(N_HEADS), blk(2 * N_HEADS),
                  pl.BlockSpec((None, N_HEADS, seq), lambda b, h: (b, 0, 0))],
        out_specs=blk(0),
        out_shape=jax.ShapeDtypeStruct((t, N_HEADS * HEAD_DIM), BF16),
        compiler_params=_params("parallel", "parallel"),
        name="fox_attention",
    )(qkv, qkv, qkv, cum)


def _mla_attention(q, kv, k_rope, *, batch, seq, tq=512):
    t = q.shape[0]
    tq = _pick(seq, tq)
    return pl.pallas_call(
        functools.partial(_softmax_attn_kernel, tq=tq, mode="mla"),
        grid=(batch, N_HEADS),
        in_specs=[pl.BlockSpec((seq, 2 * LANES), lambda b, h: (b, h)),
                  pl.BlockSpec((seq, HEAD_DIM), lambda b, h: (b, h)),
                  pl.BlockSpec((seq, LANES), lambda b, h: (b, 0)),
                  pl.BlockSpec((seq, HEAD_DIM), lambda b, h: (b, N_HEADS + h))],
        out_specs=pl.BlockSpec((seq, HEAD_DIM), lambda b, h: (b, h)),
        out_shape=jax.ShapeDtypeStruct((t, N_HEADS * HEAD_DIM), BF16),
        compiler_params=_params("parallel", "parallel"),
        name="mla_attention",
    )(q, kv, k_rope, kv)


def _sb_attn_kernel(q_ref, k_ref, v_ref, o_ref, *, tq):
    seq = q_ref.shape[0]
    row = lax.broadcasted_iota(jnp.int32, (tq, tq), 0)
    col = lax.broadcasted_iota(jnp.int32, (tq, tq), 1)
    strict = col < row
    neg_after = jnp.where(row > col, -1.0, 0.0).astype(BF16)
    neg_after2 = jnp.concatenate([neg_after, neg_after], axis=0)

    def blk(i):
        return slice(i * tq, (i + 1) * tq)

    tiles = [(qi, kj) for qi in range(seq // tq) for kj in range(qi, -1, -1)]
    scores, mids = {}, {}
    acc = later = None
    for n in range(len(tiles) + 2):
        if n < len(tiles):
            qi, kj = tiles[n]
            scores[n] = lax.dot_general(q_ref[blk(qi), :], k_ref[blk(kj), :],
                                        (((1,), (1,)), ((), ())), preferred_element_type=F32)
        if 0 <= n - 1 < len(tiles):
            qi, kj = tiles[n - 1]
            z = scores.pop(n - 1)
            sp = jnp.maximum(z, 0.0) + jnp.log(1.0 + jnp.exp2(jnp.abs(z) * -LOG2_E))
            if kj == qi:
                sp = jnp.where(strict, sp, 0.0)
            hi = sp.astype(BF16)
            lo = (sp - hi.astype(F32)).astype(BF16)
            after = jnp.dot(jnp.concatenate([hi, lo], axis=1), neg_after2,
                            preferred_element_type=F32)
            mids[n - 1] = (z - sp, after, sp[:, 0:1])
        if 0 <= n - 2 < len(tiles):
            qi, kj = tiles[n - 2]
            log_beta, after, sp0 = mids.pop(n - 2)
            if kj == qi:
                acc = jnp.zeros((tq, HEAD_DIM), F32)
                later = jnp.zeros((tq, 1), F32)
            a = jnp.exp((log_beta + after) + later)
            if kj == qi:
                a = jnp.where(strict, a, 0.0)
            acc = acc + jnp.dot(a.astype(BF16), v_ref[blk(kj), :], preferred_element_type=F32)
            later = later + (after[:, 0:1] - sp0)
            if kj == 0:
                o_ref[blk(qi), :] = acc.astype(o_ref.dtype)


def _sb_attention(qkv, *, batch, seq, tq=256):
    t = qkv.shape[0]
    tq = _pick(seq, tq)
    blk = lambda off: pl.BlockSpec((seq, HEAD_DIM), lambda b, h: (b, off + h))
    return pl.pallas_call(
        functools.partial(_sb_attn_kernel, tq=tq),
        grid=(batch, N_HEADS),
        in_specs=[blk(0), blk(N_HEADS), blk(2 * N_HEADS)],
        out_specs=blk(0),
        out_shape=jax.ShapeDtypeStruct((t, N_HEADS * HEAD_DIM), BF16),
        compiler_params=_params("parallel", "parallel"),
        name="sb_attention",
    )(qkv, qkv, qkv)


def _sgu_kernel(u_ref, v_ref, g_ref, ws_ref, bs_ref, o_ref):
    tm, width = v_ref.shape
    row = lax.broadcasted_iota(jnp.int32, (SGU_CHUNK, SGU_CHUNK), 0)
    col = lax.broadcasted_iota(jnp.int32, (SGU_CHUNK, SGU_CHUNK), 1)
    causal = col <= row
    vn = _rms(v_ref[...].astype(F32), g_ref[...]).astype(BF16)
    bs = bs_ref[...]
    for g in range(width // LANES):
        cs = slice(g * LANES, (g + 1) * LANES)
        w = jnp.where(causal, ws_ref[g], 0.0).astype(BF16)
        for n in range(tm // SGU_CHUNK):
            rs = slice(n * SGU_CHUNK, (n + 1) * SGU_CHUNK)
            mixed = jnp.dot(w, vn[rs, cs], preferred_element_type=F32) + bs[:, g:g + 1]
            o_ref[rs, cs] = (u_ref[rs, cs].astype(F32) * mixed).astype(o_ref.dtype)


def _sgu_spatial(uv, v_gain, w_s, b_s, *, tm=512):
    t = uv.shape[0]
    width = uv.shape[1] // 2
    tm = _pick(t, tm)
    return pl.pallas_call(
        _sgu_kernel,
        grid=(t // tm,),
        in_specs=[pl.BlockSpec((tm, width), lambda i: (i, 0)),
                  pl.BlockSpec((tm, width), lambda i: (i, 1)),
                  pl.BlockSpec((1, width), lambda i: (0, 0)),
                  pl.BlockSpec(w_s.shape, lambda i: (0, 0, 0)),
                  pl.BlockSpec((SGU_CHUNK, SGU_GROUPS), lambda i: (0, 0))],
        out_specs=pl.BlockSpec((tm, width), lambda i: (i, 0)),
        out_shape=jax.ShapeDtypeStruct((t, width), BF16),
        compiler_params=_params("parallel"),
        name="sgu_spatial",
    )(uv, uv, v_gain.reshape(1, width), w_s, b_s.T)


def _tile_gain(gain, reps):
    return jnp.tile(gain.astype(F32), reps)


def _qkv_mixer_proj(h, norm_gain, w_qkv, q_gain, k_gain, scale, name):
    hd = N_HEADS * HEAD_DIM
    tn = 1024
    per = tn // HEAD_DIM
    gains = jnp.stack([_tile_gain(q_gain * scale, per), _tile_gain(k_gain, per),
                       jnp.ones((tn,), F32)]).reshape(3, 1, tn)
    tiles_per_seg = hd // tn
    epi = functools.partial(_headnorm_epilogue, n_norm_tiles=2 * tiles_per_seg)
    return _norm_proj(
        h, norm_gain, w_qkv.astype(BF16), epi, BF16, k=h.shape[1], tn=tn,
        extras=(gains,),
        extra_specs=(pl.BlockSpec((None, 1, tn), lambda i, j: (j // tiles_per_seg, 0, 0)),),
        name=name)


def _fox_mixer(h, norm_gain, w_in, b_f, q_gain, k_gain, w_out, *, batch, seq):
    hd = N_HEADS * HEAD_DIM
    d = h.shape[1]
    qkv = _qkv_mixer_proj(h, norm_gain, w_in[:, :3 * hd], q_gain, k_gain, HEAD_DIM ** -0.5,
                          "fox_qkv_proj")
    w_f = jnp.pad(w_in[:, 3 * hd:], ((0, 0), (0, LANES - N_HEADS))).astype(BF16)
    f_logit = _norm_proj(h, norm_gain, w_f, _plain_epilogue, F32, k=d, tn=LANES,
                         name="fox_fgate_proj")
    b_pad = jnp.pad(b_f.astype(F32), (0, LANES - N_HEADS)).reshape(1, LANES)
    cum = _fox_cum(f_logit, b_pad, batch=batch, seq=seq)
    o = _fox_attention(qkv, cum, batch=batch, seq=seq)
    return _proj_residual(o, w_out.astype(BF16), h, name="fox_out_proj")


def _mla_mixer(h, positions, norm_gain, w_in, q_a_gain, kv_a_gain, w_q_b, w_kv_b, q_gain,
               k_gain, w_out, *, batch, seq):
    t, d = h.shape
    n_in = MLA_Q_RANK + MLA_KV_RANK + MLA_ROPE
    n_in_pad = MLA_Q_RANK + MLA_KV_RANK + LANES
    w_in_p = jnp.pad(w_in, ((0, 0), (0, n_in_pad - n_in))).astype(BF16)
    y = _norm_proj(h, norm_gain, w_in_p, _plain_epilogue, F32, k=d, tn=n_in_pad,
                   name="mla_down_proj")

    half = MLA_ROPE // 2
    inv_freq = ROPE_THETA ** (-jnp.arange(0, MLA_ROPE, 2, dtype=F32) / MLA_ROPE)
    zeros = jnp.zeros((LANES - MLA_ROPE,), F32)
    inv_row = jnp.concatenate([inv_freq, inv_freq, zeros]).reshape(1, LANES)
    cmask = jnp.concatenate([jnp.ones((MLA_ROPE,), F32), zeros]).reshape(1, LANES)
    ssign = jnp.concatenate([-jnp.ones((half,), F32), jnp.ones((half,), F32), zeros]).reshape(1, LANES)
    tm_small = _pick(t, 1024)
    row_spec = pl.BlockSpec((tm_small, LANES), lambda i: (i, 0))
    const_spec = pl.BlockSpec((1, LANES), lambda i: (0, 0))
    cos_t, sin_t = pl.pallas_call(
        _rope_table_kernel,
        grid=(t // tm_small,),
        in_specs=[pl.BlockSpec((tm_small, 1), lambda i: (i, 0)), const_spec, const_spec, const_spec],
        out_specs=[row_spec, row_spec],
        out_shape=[jax.ShapeDtypeStruct((t, LANES), F32)] * 2,
        compiler_params=_params("parallel"),
        name="rope_tables",
    )(positions.reshape(t, 1), inv_row, cmask, ssign)

    scale = (MLA_NOPE + MLA_ROPE) ** -0.5
    wq = w_q_b.reshape(MLA_Q_RANK, N_HEADS, MLA_NOPE + MLA_ROPE)
    wq = jnp.pad(wq, ((0, 0), (0, 0), (0, 2 * LANES - MLA_NOPE - MLA_ROPE)))
    wq = wq.reshape(MLA_Q_RANK, N_HEADS * 2 * LANES).astype(BF16)
    tn = 1024
    q_head_gain = jnp.concatenate([q_gain.astype(F32), zeros])
    q_gains = _tile_gain(q_head_gain, tn // (2 * LANES)).reshape(1, tn)
    tm = _pick(t, 1024)
    tab_spec = pl.BlockSpec((tm, LANES), lambda i, j: (i, 0))
    q = _norm_proj(y, q_a_gain, wq, functools.partial(_mla_q_epilogue, scale=scale), BF16,
                   k=MLA_Q_RANK, x_col_block=0, tm=tm, tn=tn,
                   extras=(q_gains, cos_t, sin_t),
                   extra_specs=(pl.BlockSpec((1, tn), lambda i, j: (0, 0)), tab_spec, tab_spec),
                   name="mla_q_proj")

    wkv = w_kv_b.reshape(MLA_KV_RANK, N_HEADS, MLA_NOPE + HEAD_DIM)
    wkv = jnp.concatenate([wkv[:, :, :MLA_NOPE].reshape(MLA_KV_RANK, -1),
                           wkv[:, :, MLA_NOPE:].reshape(MLA_KV_RANK, -1)], axis=1).astype(BF16)
    k_gains = jnp.stack([_tile_gain(k_gain[:MLA_NOPE], tn // HEAD_DIM),
                         jnp.ones((tn,), F32)]).reshape(2, 1, tn)
    tiles_per_seg = N_HEADS * HEAD_DIM // tn
    kv = _norm_proj(y, kv_a_gain, wkv,
                    functools.partial(_headnorm_epilogue, n_norm_tiles=tiles_per_seg), BF16,
                    k=MLA_KV_RANK, x_col_block=1, tn=tn,
                    extras=(k_gains,),
                    extra_specs=(pl.BlockSpec((None, 1, tn), lambda i, j: (j // tiles_per_seg, 0, 0)),),
                    name="mla_kv_proj")

    kr_gain = jnp.concatenate([k_gain[MLA_NOPE:].astype(F32), zeros]).reshape(1, LANES)
    k_rope = pl.pallas_call(
        _krope_kernel,
        grid=(t // tm_small,),
        in_specs=[pl.BlockSpec((tm_small, LANES), lambda i: (i, (MLA_Q_RANK + MLA_KV_RANK) // LANES)),
                  const_spec, row_spec, row_spec],
        out_specs=row_spec,
        out_shape=jax.ShapeDtypeStruct((t, LANES), BF16),
        compiler_params=_params("parallel"),
        name="mla_k_rope",
    )(y, kr_gain, cos_t, sin_t)

    o = _mla_attention(q, kv, k_rope, batch=batch, seq=seq)
    return _proj_residual(o, w_out.astype(BF16), h, name="mla_out_proj")


def _sb_mixer(h, norm_gain, w_in, q_gain, k_gain, w_out, *, batch, seq):
    qkv = _qkv_mixer_proj(h, norm_gain, w_in, q_gain, k_gain, HEAD_DIM ** -0.5, "sb_qkv_proj")
    o = _sb_attention(qkv, batch=batch, seq=seq)
    return _proj_residual(o, w_out.astype(BF16), h, name="sb_out_proj")


def _sgu_mixer(h, norm_gain, w_in, v_gain, w_s, b_s, w_out):
    uv = _norm_proj(h, norm_gain, w_in.astype(BF16), _gelu_epilogue, BF16, k=h.shape[1],
                    name="sgu_in_proj")
    gated = _sgu_spatial(uv, v_gain, w_s, b_s)
    return _proj_residual(gated, w_out.astype(BF16), h, name="sgu_out_proj")


def kernel(x, positions, mix_norm, ffn_norm, fox_w_in, fox_b_f, fox_q_gain, fox_k_gain, fox_w_out, mla_w_in, mla_q_a_gain, mla_kv_a_gain, mla_w_q_b, mla_w_kv_b, mla_q_gain, mla_k_gain, mla_w_out, sb_w_in, sb_q_gain, sb_k_gain, sb_w_out, sgu_w_in, sgu_v_gain, sgu_w_s, sgu_b_s, sgu_w_out, ffn_w_up, ffn_conv_w, ffn_conv_b, ffn_w_down):
    batch, seq, d = x.shape
    depth = mix_norm.shape[0]
    n_mixers = 4
    h = x.reshape(batch * seq, d)
    w_up_all = ffn_w_up.astype(BF16)
    w_down_all = ffn_w_down.astype(BF16)
    for i in range(depth):
        m, j = i % n_mixers, i // n_mixers
        if m == 0:
            h = _fox_mixer(h, mix_norm[i], fox_w_in[j], fox_b_f[j], fox_q_gain[j], fox_k_gain[j],
                           fox_w_out[j], batch=batch, seq=seq)
        elif m == 1:
            h = _mla_mixer(h, positions, mix_norm[i], mla_w_in[j], mla_q_a_gain[j],
                           mla_kv_a_gain[j], mla_w_q_b[j], mla_w_kv_b[j], mla_q_gain[j],
                           mla_k_gain[j], mla_w_out[j], batch=batch, seq=seq)
        elif m == 2:
            h = _sb_mixer(h, mix_norm[i], sb_w_in[j], sb_q_gain[j], sb_k_gain[j], sb_w_out[j],
                          batch=batch, seq=seq)
        else:
            h = _sgu_mixer(h, mix_norm[i], sgu_w_in[j], sgu_v_gain[j], sgu_w_s[j], sgu_b_s[j],
                           sgu_w_out[j])
        h = _ffn(h, ffn_norm[i], w_up_all, ffn_conv_w[i], ffn_conv_b[i], w_down_all,
                 layer=i, seq=seq)
    return h.reshape(batch, seq, d)
```
